```python
import math
import jax, jax.numpy as jnp
from jax import lax
import numpy as np

D_MODEL = 4096
BATCH = 8
SEQ = 2048
DEPTH = 2

N_A = DEPTH // 2
N_B = DEPTH - N_A
HEAD_DIM = 128
POOL_WINDOWS = (2, 4, 8, 16)
N_POOL_GROUPS = 4
POOL_WIDTH = 3 * D_MODEL // 4
POOL_GROUP = POOL_WIDTH // N_POOL_GROUPS
MEM_LEN = 256
MEM_HEADS = 4
MEM_HEAD_DIM = D_MODEL // 16
MEM_WIDTH = MEM_HEADS * MEM_HEAD_DIM
DIL_CONFIGS = ((128, 1), (512, 4), (2048, 16))
N_DIL_GROUPS = len(DIL_CONFIGS)
DIL_HEADS = D_MODEL // 512
DIL_Q_WIDTH = N_DIL_GROUPS * DIL_HEADS * HEAD_DIM
DIL_OUT_WIDTH = DIL_HEADS * HEAD_DIM
KV_WIDTH = 2 * DIL_Q_WIDTH
A_IN_WIDTH = POOL_WIDTH + MEM_WIDTH
A_OUT_WIDTH = POOL_WIDTH + MEM_WIDTH
B_IN_WIDTH = DIL_Q_WIDTH + MEM_WIDTH
B_OUT_WIDTH = DIL_OUT_WIDTH + MEM_WIDTH
D_FF = 4 * D_MODEL
NUM_BUCKETS = 32
MAX_DISTANCE = 2048
EPS = 1e-6

kernel_name = "yoco_pool_dilated_hybrid"


def rmsnorm(x, g):
    x32 = x.astype(jnp.float32)
    y = x32 * lax.rsqrt(jnp.mean(x32 * x32, axis=-1, keepdims=True) + EPS)
    return (y * g.astype(jnp.float32)).astype(x.dtype)


def sq_relu_mlp(h, w1, w2):
    a = jax.nn.relu(h @ w1)
    return (a * a) @ w2


def t5_bucket(dist):
    max_exact = NUM_BUCKETS // 2
    d32 = jnp.maximum(dist, 1).astype(jnp.float32)
    large = max_exact + (jnp.log(d32 / max_exact) / math.log(MAX_DISTANCE / max_exact)
                         * (NUM_BUCKETS - max_exact)).astype(jnp.int32)
    large = jnp.minimum(large, NUM_BUCKETS - 1)
    return jnp.where(dist < max_exact, dist, large)


def pool_mixer(u, w_pg, scale):
    b, s, _ = u.shape
    ug = u.reshape(b, s, N_POOL_GROUPS, POOL_GROUP).astype(jnp.float32)
    cs = jnp.cumsum(ug, axis=1)
    pos = jnp.arange(s)
    outs = []
    for g, w in enumerate(POOL_WINDOWS):
        c = cs[:, :, g]
        lag = jnp.pad(c, ((0, 0), (w, 0), (0, 0)))[:, :s]
        cnt = jnp.minimum(pos + 1, w).astype(jnp.float32)[None, :, None]
        outs.append((c - lag) / cnt - ug[:, :, g])
    pooled = jnp.stack(outs, axis=2).astype(u.dtype)
    mixed = jnp.einsum('bsgc,gcd->bsgd', pooled, w_pg).reshape(b, s, POOL_WIDTH)
    return mixed * scale


def memory_attention(u_mem, mk, mv):
    b, s, _ = u_mem.shape
    q = u_mem.reshape(b, s, MEM_HEADS, MEM_HEAD_DIM)
    logits = jnp.einsum('bshc,bmhc->bhsm', q, mk).astype(jnp.float32) / math.sqrt(MEM_HEAD_DIM)
    p = jax.nn.softmax(logits, axis=-1).astype(mv.dtype)
    o = jnp.einsum('bhsm,bmhc->bshc', p, mv)
    return o.reshape(b, s, MEM_WIDTH)


def dilated_group(q, k, v, window, dil, bias_g):
    b, s, h, hd = q.shape
    wd = window // dil
    blk = wd
    L = s // dil
    nblk = -(-L // blk)
    lp = nblk * blk

    def to_blocks(t):
        t = t.reshape(b, L, dil, h, hd).transpose(0, 3, 2, 1, 4)
        t = jnp.pad(t, ((0, 0), (0, 0), (0, 0), (0, lp - L), (0, 0)))
        return t.reshape(b, h, dil, nblk, blk, hd)

    def with_prev(t):
        prev = jnp.pad(t, ((0, 0), (0, 0), (0, 0), (1, 0), (0, 0), (0, 0)))[:, :, :, :nblk]
        return jnp.concatenate([prev, t], axis=4)

    qb = to_blocks(q)
    kk = with_prev(to_blocks(k))
    vv = with_prev(to_blocks(v))

    qi = jnp.arange(blk)[:, None]
    kj = jnp.arange(2 * blk)[None, :]
    delta = qi + blk - kj
    band = (delta >= 0) & (delta <= wd)
    first = (jnp.arange(nblk)[:, None, None] == 0) & (kj[None] < blk)
    valid = band[None] & ~first
    bucket = t5_bucket(jnp.maximum(delta, 0) * dil)
    bias = bias_g[bucket].astype(jnp.float32).transpose(2, 0, 1)

    logits = jnp.einsum('bhrnqc,bhrnkc->bhrnqk', qb, kk).astype(jnp.float32) / math.sqrt(hd)
    logits = logits + bias[None, :, None, None]
    logits = jnp.where(valid[None, None, None], logits, -jnp.inf)
    m = jnp.max(logits, axis=-1, keepdims=True)
    p = jnp.exp(logits - m)
    den = jnp.sum(p, axis=-1, keepdims=True)
    o = jnp.einsum('bhrnqk,bhrnkc->bhrnqc', (p / den).astype(v.dtype), vv)
    lse = (m + jnp.log(den))[..., 0]

    o = o.reshape(b, h, dil, lp, hd)[:, :, :, :L].transpose(0, 3, 2, 1, 4).reshape(b, s, h, hd)
    lse = lse.reshape(b, h, dil, lp)[..., :L].transpose(0, 3, 2, 1).reshape(b, s, h)
    return o, lse


def dilated_attention(q, k, v, rel_bias):
    b, s = q.shape[:2]
    outs, lses = [], []
    for g, (window, dil) in enumerate(DIL_CONFIGS):
        bias_g = rel_bias[:, g * DIL_HEADS:(g + 1) * DIL_HEADS]
        o, lse = dilated_group(q[:, :, g], k[:, :, g], v[:, :, g], window, dil, bias_g)
        outs.append(o)
        lses.append(lse)
    wgt = jax.nn.softmax(jnp.stack(lses, axis=-1), axis=-1)
    o = jnp.stack(outs, axis=-1).astype(jnp.float32)
    o = jnp.sum(o * wgt[:, :, :, None, :], axis=-1).astype(q.dtype)
    return o.reshape(b, s, DIL_OUT_WIDTH)


def setup_inputs(seed: int = 0) -> dict:
    key = jax.random.key(seed)
    ks = jax.random.split(key, 24)
    f32 = jnp.float32

    def nrm(k, shape, fan_in):
        return jax.random.normal(k, shape, f32) * (fan_in ** -0.5)

    def gain(k, shape):
        return 1.0 + 0.02 * jax.random.normal(k, shape, f32)

    return {
        "x": jax.random.normal(ks[0], (BATCH, SEQ, D_MODEL), f32),
        "mem": jax.random.normal(ks[1], (BATCH, MEM_LEN, D_MODEL), f32),
        "a_norm": gain(ks[2], (N_A, D_MODEL)),
        "a_w_in": nrm(ks[3], (N_A, D_MODEL, A_IN_WIDTH), D_MODEL),
        "a_w_pg": nrm(ks[4], (N_A, N_POOL_GROUPS, POOL_GROUP, POOL_GROUP), POOL_GROUP),
        "a_scale": gain(ks[5], (N_A, POOL_WIDTH)),
        "a_w_out": nrm(ks[6], (N_A, A_OUT_WIDTH, D_MODEL), A_OUT_WIDTH),
        "kv_norm": gain(ks[7], (D_MODEL,)),
        "w_kv": nrm(ks[8], (D_MODEL, KV_WIDTH), D_MODEL),
        "b_norm": gain(ks[9], (N_B, D_MODEL)),
        "b_w_in": nrm(ks[10], (N_B, D_MODEL, B_IN_WIDTH), D_MODEL),
        "b_w_out": nrm(ks[11], (N_B, B_OUT_WIDTH, D_MODEL), B_OUT_WIDTH),
        "mem_norm": gain(ks[12], (D_MODEL,)),
        "w_mem_kv": nrm(ks[13], (DEPTH, D_MODEL, 2 * MEM_WIDTH), D_MODEL),
        "mlp_norm": gain(ks[14], (DEPTH, D_MODEL)),
        "mlp_w1": nrm(ks[15], (DEPTH, D_MODEL, D_FF), D_MODEL),
        "mlp_w2": nrm(ks[16], (DEPTH, D_FF, D_MODEL), D_FF),
        "rel_bias": 0.2 * jax.random.normal(ks[17], (NUM_BUCKETS, N_DIL_GROUPS * DIL_HEADS), f32),
        "final_norm": gain(ks[18], (D_MODEL,)),
    }


def reference(x, mem, a_norm, a_w_in, a_w_pg, a_scale, a_w_out, kv_norm, w_kv,
              b_norm, b_w_in, b_w_out, mem_norm, w_mem_kv, mlp_norm, mlp_w1, mlp_w2,
              rel_bias, final_norm):
    b, s, _ = x.shape
    mem_h = rmsnorm(mem, mem_norm)
    k_sh = None
    v_sh = None
    for l in range(DEPTH):
        mkv = (mem_h @ w_mem_kv[l]).reshape(b, MEM_LEN, 2, MEM_HEADS, MEM_HEAD_DIM)
        mk, mv = mkv[:, :, 0], mkv[:, :, 1]
        if l < N_A:
            h = rmsnorm(x, a_norm[l])
            u = h @ a_w_in[l]
            pool_out = pool_mixer(u[..., :POOL_WIDTH], a_w_pg[l], a_scale[l])
            mem_out = memory_attention(u[..., POOL_WIDTH:], mk, mv)
            x = x + jnp.concatenate([pool_out, mem_out], axis=-1) @ a_w_out[l]
        else:
            i = l - N_A
            if i == 0:
                kv = (rmsnorm(x, kv_norm) @ w_kv).reshape(b, s, 2, N_DIL_GROUPS, DIL_HEADS, HEAD_DIM)
                k_sh, v_sh = kv[:, :, 0], kv[:, :, 1]
            h = rmsnorm(x, b_norm[i])
            u = h @ b_w_in[i]
            q = u[..., :DIL_Q_WIDTH].reshape(b, s, N_DIL_GROUPS, DIL_HEADS, HEAD_DIM)
            dil_out = dilated_attention(q, k_sh, v_sh, rel_bias)
            mem_out = memory_attention(u[..., DIL_Q_WIDTH:], mk, mv)
            x = x + jnp.concatenate([dil_out, mem_out], axis=-1) @ b_w_out[i]
        x = x + sq_relu_mlp(rmsnorm(x, mlp_norm[l]), mlp_w1[l], mlp_w2[l])
    return rmsnorm(x, final_norm)
```

```python
import functools
import math

import jax
import jax.numpy as jnp
from jax import lax
from jax.experimental import pallas as pl
from jax.experimental.pallas import tpu as pltpu

F32 = jnp.float32
BF16 = jnp.bfloat16

EPS = 1e-6
POOL_WINDOWS = (2, 4, 8, 16)
POOL_HALO = 16
MEM_HEADS = 4
HEAD_DIM = 128
DIL_CONFIGS = ((128, 1), (512, 4), (2048, 16))
ATT_BLOCK = 128
NUM_BUCKETS = 32
MAX_DISTANCE = 2048
MASKED = -1e30

V7X_VMEM_BYTES = 64 * 1024 * 1024
VMEM_LIMIT_BYTES = V7X_VMEM_BYTES - 8 * 1024 * 1024


def _params(*semantics):
    return pltpu.CompilerParams(dimension_semantics=semantics, vmem_limit_bytes=VMEM_LIMIT_BYTES)


def _rmsnorm_kernel(x_ref, g_ref, *o_refs):
    x = x_ref[...]
    y = x * lax.rsqrt(jnp.mean(x * x, axis=-1, keepdims=True) + EPS)
    for n, o_ref in enumerate(o_refs):
        o_ref[...] = (y * g_ref[n:n + 1, :]).astype(o_ref.dtype)


def _rmsnorm(x, gains, out_dtype, tm=256):
    m, d = x.shape
    n = len(gains)
    g = jnp.stack(gains).astype(F32)
    outs = pl.pallas_call(
        _rmsnorm_kernel,
        grid=(m // tm,),
        in_specs=[pl.BlockSpec((tm, d), lambda i: (i, 0)),
                  pl.BlockSpec((n, d), lambda i: (0, 0))],
        out_specs=[pl.BlockSpec((tm, d), lambda i: (i, 0))] * n,
        out_shape=[jax.ShapeDtypeStruct((m, d), out_dtype)] * n,
        compiler_params=_params("parallel"),
        name="rmsnorm",
    )(x, g)
    return list(outs)


def _linear_kernel(*refs, seg_starts, seg_counts, nk, act, has_res):
    n_lhs = len(seg_starts)
    lhs_refs = refs[:n_lhs]
    w_ref = refs[n_lhs]
    res_ref = refs[n_lhs + 1] if has_res else None
    o_ref = refs[n_lhs + 1 + int(has_res)]
    acc_ref = refs[n_lhs + 2 + int(has_res)]
    k = pl.program_id(2)

    @pl.when(k == 0)
    def _():
        acc_ref[...] = jnp.zeros_like(acc_ref)

    for lhs_ref, start, count in zip(lhs_refs, seg_starts, seg_counts):
        def _accumulate(lhs_ref=lhs_ref):
            acc_ref[...] += jnp.dot(lhs_ref[...], w_ref[...], preferred_element_type=F32)
        if n_lhs == 1:
            _accumulate()
        else:
            pl.when((k >= start) & (k < start + count))(_accumulate)

    @pl.when(k == nk - 1)
    def _():
        y = acc_ref[...]
        if act == "relu2":
            y = jnp.maximum(y, 0.0)
            y = y * y
        if has_res:
            y = res_ref[...] + y
        o_ref[...] = y.astype(o_ref.dtype)


def _linear(lhs_list, w, n_out, out_dtype, *, res=None, act=None, tm=1024, tn=1024, tk=1024,
            name="linear"):
    m = lhs_list[0].shape[0]
    tm, tn = math.gcd(tm, m), math.gcd(tn, n_out)
    tk = math.gcd(tk, *[a.shape[1] for a in lhs_list])
    seg_counts = [a.shape[1] // tk for a in lhs_list]
    seg_starts = [sum(seg_counts[:i]) for i in range(len(seg_counts))]
    nk = sum(seg_counts)
    assert all(a.shape[1] % tk == 0 for a in lhs_list) and nk * tk == w.shape[0]
    assert m % tm == 0 and n_out % tn == 0

    def lhs_spec(start, count):
        return pl.BlockSpec((tm, tk), lambda i, j, k: (i, jnp.clip(k - start, 0, count - 1)))

    in_specs = [lhs_spec(s, c) for s, c in zip(seg_starts, seg_counts)]
    in_specs.append(pl.BlockSpec((tk, tn), lambda i, j, k: (k, j)))
    args = list(lhs_list) + [w]
    if res is not None:
        in_specs.append(pl.BlockSpec((tm, tn), lambda i, j, k: (i, j)))
        args.append(res)
    kernel = functools.partial(_linear_kernel, seg_starts=tuple(seg_starts),
                               seg_counts=tuple(seg_counts), nk=nk, act=act,
                               has_res=res is not None)
    return pl.pallas_call(
        kernel,
        grid=(m // tm, n_out // tn, nk),
        in_specs=in_specs,
        out_specs=pl.BlockSpec((tm, tn), lambda i, j, k: (i, j)),
        out_shape=jax.ShapeDtypeStruct((m, n_out), out_dtype),
        scratch_shapes=[pltpu.VMEM((tm, tn), F32)],
        compiler_params=_params("parallel", "parallel", "arbitrary"),
        name=name,
    )(*args)


def _pool_kernel(h_ref, halo_ref, w_ref, wpg_ref, scale_ref, o_ref, u_ref, p_ref, *, tm):
    i = pl.program_id(1)
    g = pl.program_id(2)
    w = w_ref[...]
    u_ref[POOL_HALO:, :] = jnp.dot(h_ref[...], w, preferred_element_type=F32)
    u_halo = jnp.dot(halo_ref[...], w, preferred_element_type=F32)
    u_ref[:POOL_HALO, :] = jnp.where(i > 0, u_halo, 0.0)
    pos = i * tm + lax.broadcasted_iota(jnp.int32, (tm, 1), 0)

    for group, window in enumerate(POOL_WINDOWS):
        @pl.when(g == group)
        def _(window=window):
            u = u_ref[...]
            s = u
            shift = 1
            while shift < window:
                s = s + pltpu.roll(s, shift, axis=0)
                shift *= 2
            inv_cnt = 1.0 / jnp.minimum(pos + 1, window).astype(F32)
            p_ref[...] = (s[POOL_HALO:] * inv_cnt - u[POOL_HALO:]).astype(p_ref.dtype)

    mixed = jnp.dot(p_ref[...], wpg_ref[...], preferred_element_type=F32) * scale_ref[...]
    o_ref[...] = mixed.astype(o_ref.dtype)


def _pool_branch(h, w_in, w_pg, scale, batch, seq, tm=1024):
    n, d = h.shape
    groups, gw = w_pg.shape[0], w_pg.shape[1]
    tm = min(tm, seq)
    n_i = seq // tm
    halo_per_tm = tm // POOL_HALO

    def halo_map(b, i, g):
        return (jnp.maximum((b * n_i + i) * halo_per_tm - 1, 0), 0)

    return pl.pallas_call(
        functools.partial(_pool_kernel, tm=tm),
        grid=(batch, n_i, groups),
        in_specs=[pl.BlockSpec((tm, d), lambda b, i, g: (b * n_i + i, 0)),
                  pl.BlockSpec((POOL_HALO, d), halo_map),
                  pl.BlockSpec((d, gw), lambda b, i, g: (0, g)),
                  pl.BlockSpec((None, gw, gw), lambda b, i, g: (g, 0, 0)),
                  pl.BlockSpec((1, gw), lambda b, i, g: (0, g))],
        out_specs=pl.BlockSpec((tm, gw), lambda b, i, g: (b * n_i + i, g)),
        out_shape=jax.ShapeDtypeStruct((n, groups * gw), BF16),
        scratch_shapes=[pltpu.VMEM((POOL_HALO + tm, gw), F32), pltpu.VMEM((tm, gw), BF16)],
        compiler_params=_params("parallel", "parallel", "arbitrary"),
        name="pool_branch",
    )(h, h, w_in, w_pg, scale.reshape(1, -1))


def _mem_attn_kernel(h_ref, wq_ref, k_ref, v_ref, o_ref, *, scale):
    q = jnp.dot(h_ref[...], wq_ref[...], preferred_element_type=F32).astype(BF16)
    s = lax.dot_general(q, k_ref[...], (((1,), (1,)), ((), ())), preferred_element_type=F32) * scale
    m = jnp.max(s, axis=-1, keepdims=True)
    p = jnp.exp(s - m)
    inv_den = 1.0 / jnp.sum(p, axis=-1, keepdims=True)
    o = jnp.dot(p.astype(BF16), v_ref[...], preferred_element_type=F32) * inv_den
    o_ref[...] = o.astype(o_ref.dtype)


def _mem_attn(h, w_in, q_col0, mkv, batch, seq, mem_len, tm=1024):
    n, d = h.shape
    hd = mkv.shape[1] // (2 * MEM_HEADS)
    tm = min(tm, seq)
    n_i = seq // tm
    q_blk0 = q_col0 // hd
    assert q_col0 % hd == 0
    return pl.pallas_call(
        functools.partial(_mem_attn_kernel, scale=1.0 / math.sqrt(hd)),
        grid=(batch, n_i, MEM_HEADS),
        in_specs=[pl.BlockSpec((tm, d), lambda b, i, hh: (b * n_i + i, 0)),
                  pl.BlockSpec((d, hd), lambda b, i, hh: (0, q_blk0 + hh)),
                  pl.BlockSpec((mem_len, hd), lambda b, i, hh: (b, hh)),
                  pl.BlockSpec((mem_len, hd), lambda b, i, hh: (b, MEM_HEADS + hh))],
        out_specs=pl.BlockSpec((tm, hd), lambda b, i, hh: (b * n_i + i, hh)),
        out_shape=jax.ShapeDtypeStruct((n, MEM_HEADS * hd), BF16),
        compiler_params=_params("parallel", "parallel", "arbitrary"),
        name="mem_attn",
    )(h, w_in, mkv, mkv)


def _t5_bucket(dist):
    max_exact = NUM_BUCKETS // 2
    d32 = jnp.maximum(dist, 1).astype(F32)
    large = max_exact + (jnp.log(d32 / max_exact) / math.log(MAX_DISTANCE / max_exact)
                         * (NUM_BUCKETS - max_exact)).astype(jnp.int32)
    large = jnp.minimum(large, NUM_BUCKETS - 1)
    return jnp.where(dist < max_exact, dist, large)


def _bucket_tables():
    qi = jnp.arange(ATT_BLOCK)[:, None]
    kj = jnp.arange(2 * ATT_BLOCK)[None, :]
    delta = qi + ATT_BLOCK - kj
    band = (delta >= 0) & (delta <= ATT_BLOCK)
    tabs = [jnp.where(band, _t5_bucket(jnp.maximum(delta, 0) * dil), -1) for _, dil in DIL_CONFIGS]
    return jnp.stack(tabs).astype(jnp.int32)


def _dil_attn_kernel(rb_ref, bkt_ref, q0, q1, q2, k0, k1, k2, v0, v1, v2, o_ref,
                     og0, og1, og2, lg0, lg1, lg2, *, seq, heads, scale):
    head = pl.program_id(1)
    q_refs, k_refs, v_refs = (q0, q1, q2), (k0, k1, k2), (v0, v1, v2)
    o_nat, lse_nat = (og0, og1, og2), (lg0, lg1, lg2)
    blk = ATT_BLOCK

    for g, (_, dil) in enumerate(DIL_CONFIGS):
        sub_len = seq // dil
        n_blk = sub_len // blk
        bkt = bkt_ref[g]
        bias = jnp.full(bkt.shape, MASKED, F32)
        for b in range(NUM_BUCKETS):
            bias = jnp.where(bkt == b, rb_ref[b, g * heads + head], bias)

        for r in range(dil):
            rows = pl.ds(r, sub_len, stride=dil) if dil > 1 else pl.ds(0, sub_len)
            q_r = q_refs[g][rows, :].astype(BF16)
            k_r = k_refs[g][rows, :].astype(BF16)
            v_r = v_refs[g][rows, :].astype(BF16)
            for n in range(n_blk):
                lo = max(n - 1, 0) * blk
                qb = q_r[n * blk:(n + 1) * blk]
                kk = k_r[lo:(n + 1) * blk]
                vv = v_r[lo:(n + 1) * blk]
                tb = bias if n > 0 else bias[:, blk:]
                s = lax.dot_general(qb, kk, (((1,), (1,)), ((), ())),
                                    preferred_element_type=F32) * scale + tb
                m = jnp.max(s, axis=-1, keepdims=True)
                p = jnp.exp(s - m)
                den = jnp.sum(p, axis=-1, keepdims=True)
                o = jnp.dot(p.astype(BF16), vv, preferred_element_type=F32) * (1.0 / den)
                lse = m + jnp.log(den)
                start = r + dil * n * blk
                nat = pl.ds(start, blk, stride=dil) if dil > 1 else pl.ds(start, blk)
                o_nat[g][nat, :] = o
                lse_nat[g][nat, :] = jnp.broadcast_to(lse, (blk, HEAD_DIM))

    l0, l1, l2 = lse_nat[0][...], lse_nat[1][...], lse_nat[2][...]
    top = jnp.maximum(jnp.maximum(l0, l1), l2)
    w0, w1, w2 = jnp.exp(l0 - top), jnp.exp(l1 - top), jnp.exp(l2 - top)
    inv = 1.0 / (w0 + w1 + w2)
    out = (o_nat[0][...] * w0 + o_nat[1][...] * w1 + o_nat[2][...] * w2) * inv
    o_ref[...] = out.astype(o_ref.dtype)


def _dil_attn(q, kv, rel_bias, batch, seq):
    n = q.shape[0]
    n_groups = len(DIL_CONFIGS)
    heads = rel_bias.shape[1] // n_groups
    gh = n_groups * heads

    def col_spec(off):
        return pl.BlockSpec((seq, HEAD_DIM), lambda b, hh: (b, off + hh))

    in_specs = [pl.BlockSpec(memory_space=pltpu.SMEM),
                pl.BlockSpec((n_groups, ATT_BLOCK, 2 * ATT_BLOCK), lambda b, hh: (0, 0, 0))]
    in_specs += [col_spec(g * heads) for g in range(n_groups)]
    in_specs += [col_spec(g * heads) for g in range(n_groups)]
    in_specs += [col_spec(gh + g * heads) for g in range(n_groups)]
    return pl.pallas_call(
        functools.partial(_dil_attn_kernel, seq=seq, heads=heads, scale=1.0 / math.sqrt(HEAD_DIM)),
        grid=(batch, heads),
        in_specs=in_specs,
        out_specs=pl.BlockSpec((seq, HEAD_DIM), lambda b, hh: (b, hh)),
        out_shape=jax.ShapeDtypeStruct((n, heads * HEAD_DIM), BF16),
        scratch_shapes=[pltpu.VMEM((seq, HEAD_DIM), F32)] * (2 * n_groups),
        compiler_params=_params("parallel", "arbitrary"),
        name="dil_attn",
    )(rel_bias.astype(F32), _bucket_tables(), q, q, q, kv, kv, kv, kv, kv, kv)


def kernel(x, mem, a_norm, a_w_in, a_w_pg, a_scale, a_w_out, kv_norm, w_kv, b_norm, b_w_in,
           b_w_out, mem_norm, w_mem_kv, mlp_norm, mlp_w1, mlp_w2, rel_bias, final_norm):
    batch, seq, d = x.shape
    mem_len = mem.shape[1]
    n = batch * seq
    pool_width = a_scale.shape[-1]
    dil_q_width = w_kv.shape[1] // 2
    assert a_norm.shape[0] == 1 and b_norm.shape[0] == 1 and mlp_norm.shape[0] == 2

    def bf(w):
        return w.astype(BF16)

    def mlp(xs, layer):
        (hm,) = _rmsnorm(xs, [mlp_norm[layer]], BF16)
        hid = _linear([hm], bf(mlp_w1[layer]), mlp_w1.shape[2], BF16, act="relu2", name="mlp_up")
        return _linear([hid], bf(mlp_w2[layer]), d, F32, res=xs, name="mlp_down")

    x0 = x.reshape(n, d)
    (mem_h,) = _rmsnorm(mem.reshape(batch * mem_len, d), [mem_norm], BF16)
    mkv = [_linear([mem_h], bf(w_mem_kv[l]), w_mem_kv.shape[2], BF16, name="mem_kv")
           for l in range(2)]

    (h0,) = _rmsnorm(x0, [a_norm[0]], BF16)
    a_in = bf(a_w_in[0])
    mixed = _pool_branch(h0, a_in, bf(a_w_pg[0]), a_scale[0], batch, seq)
    mem_out = _mem_attn(h0, a_in, pool_width, mkv[0], batch, seq, mem_len)
    x1 = _linear([mixed, mem_out], bf(a_w_out[0]), d, F32, res=x0, name="a_out")
    x2 = mlp(x1, 0)

    h_kv, h_b = _rmsnorm(x2, [kv_norm, b_norm[0]], BF16)
    kv = _linear([h_kv], bf(w_kv), w_kv.shape[1], F32, name="kv_proj")
    b_in = bf(b_w_in[0])
    q = _linear([h_b], b_in, dil_q_width, F32, name="q_proj")
    dil_out = _dil_attn(q, kv, rel_bias, batch, seq)
    mem_out = _mem_attn(h_b, b_in, dil_q_width, mkv[1], batch, seq, mem_len)
    x3 = _linear([dil_out, mem_out], bf(b_w_out[0]), d, F32, res=x2, name="b_out")
    x4 = mlp(x3, 1)

    (out,) = _rmsnorm(x4, [final_norm], F32)
    return out.reshape(batch, seq, d)
```

```python
import functools
import math

import jax
import jax.numpy as jnp
from jax import lax
from jax.experimental import pallas as pl
from jax.experimental.pallas import tpu as pltpu

F32 = jnp.float32
BF16 = jnp.bfloat16

EPS = 1e-6
POOL_WINDOWS = (2, 4, 8, 16)
POOL_HALO = 16
MEM_HEADS = 4
HEAD_DIM = 128
DIL_CONFIGS = ((128, 1), (512, 4), (2048, 16))
ATT_BLOCK = 128
NUM_BUCKETS = 32
MAX_DISTANCE = 2048
MASKED = -1e30

V7X_VMEM_BYTES = 64 * 1024 * 1024
VMEM_LIMIT_BYTES = V7X_VMEM_BYTES - 8 * 1024 * 1024


def _params(*semantics):
    return pltpu.CompilerParams(dimension_semantics=semantics, vmem_limit_bytes=VMEM_LIMIT_BYTES)


def _rmsnorm_kernel(x_ref, g_ref, *o_refs):
    x = x_ref[...]
    y = x * lax.rsqrt(jnp.mean(x * x, axis=-1, keepdims=True) + EPS)
    for n, o_ref in enumerate(o_refs):
        o_ref[...] = (y * g_ref[n:n + 1, :]).astype(o_ref.dtype)


def _rmsnorm(x, gains, out_dtype, tm=256):
    m, d = x.shape
    n = len(gains)
    g = jnp.stack(gains).astype(F32)
    outs = pl.pallas_call(
        _rmsnorm_kernel,
        grid=(m // tm,),
        in_specs=[pl.BlockSpec((tm, d), lambda i: (i, 0)),
                  pl.BlockSpec((n, d), lambda i: (0, 0))],
        out_specs=[pl.BlockSpec((tm, d), lambda i: (i, 0))] * n,
        out_shape=[jax.ShapeDtypeStruct((m, d), out_dtype)] * n,
        compiler_params=_params("parallel"),
        name="rmsnorm",
    )(x, g)
    return list(outs)


def _linear_kernel(*refs, k_sizes, nk, act, has_res):
    n_lhs = len(k_sizes)
    lhs_refs = refs[:n_lhs]
    w_ref = refs[n_lhs]
    res_ref = refs[n_lhs + 1] if has_res else None
    o_ref = refs[n_lhs + 1 + int(has_res)]

    def product():
        y, off = None, 0
        for lhs_ref, ks in zip(lhs_refs, k_sizes):
            t = jnp.dot(lhs_ref[...], w_ref[off:off + ks, :], preferred_element_type=F32)
            y = t if y is None else y + t
            off += ks
        return y

    def finish(y):
        if act == "relu2":
            y = jnp.maximum(y, 0.0)
            y = y * y
        if has_res:
            y = res_ref[...] + y
        o_ref[...] = y.astype(o_ref.dtype)

    if nk == 1:
        finish(product())
        return

    acc_ref = refs[n_lhs + 2 + int(has_res)]
    k = pl.program_id(2)

    @pl.when(k == 0)
    def _():
        acc_ref[...] = jnp.zeros_like(acc_ref)

    acc_ref[...] += product()

    @pl.when(k == nk - 1)
    def _():
        finish(acc_ref[...])


def _linear(lhs_list, w, n_out, out_dtype, *, res=None, act=None, tm=1024, tn=1024, tk=4096,
            name="linear"):
    m = lhs_list[0].shape[0]
    k_sizes = [a.shape[1] for a in lhs_list]
    k_total = sum(k_sizes)
    assert k_total == w.shape[0]
    tm, tn = math.gcd(tm, m), math.gcd(tn, n_out)
    if len(lhs_list) > 1:
        assert k_total <= tk
        tk = k_total
    else:
        tk = math.gcd(tk, k_total)
        k_sizes = [tk]
    nk = k_total // tk

    in_specs = [pl.BlockSpec((tm, ks), lambda i, j, k: (i, k)) for ks in k_sizes]
    in_specs.append(pl.BlockSpec((tk, tn), lambda i, j, k: (k, j)))
    args = list(lhs_list) + [w]
    if res is not None:
        in_specs.append(pl.BlockSpec((tm, tn), lambda i, j, k: (i, j)))
        args.append(res)
    kernel = functools.partial(_linear_kernel, k_sizes=tuple(k_sizes), nk=nk, act=act,
                               has_res=res is not None)
    return pl.pallas_call(
        kernel,
        grid=(m // tm, n_out // tn, nk),
        in_specs=in_specs,
        out_specs=pl.BlockSpec((tm, tn), lambda i, j, k: (i, j)),
        out_shape=jax.ShapeDtypeStruct((m, n_out), out_dtype),
        scratch_shapes=[pltpu.VMEM((tm, tn), F32)] if nk > 1 else [],
        compiler_params=_params("parallel", "parallel", "arbitrary"),
        name=name,
    )(*args)


def _pool_kernel(h_ref, halo_ref, w_ref, wpg_ref, scale_ref, o_ref, u_ref, p_ref, *, tm):
    i = pl.program_id(1)
    g = pl.program_id(2)
    w = w_ref[...]
    u_ref[POOL_HALO:, :] = jnp.dot(h_ref[...], w, preferred_element_type=F32)
    u_halo = jnp.dot(halo_ref[...], w, preferred_element_type=F32)
    u_ref[:POOL_HALO, :] = jnp.where(i > 0, u_halo, 0.0)
    pos = i * tm + lax.broadcasted_iota(jnp.int32, (tm, 1), 0)

    for group, window in enumerate(POOL_WINDOWS):
        @pl.when(g == group)
        def _(window=window):
            u = u_ref[...]
            s = u
            shift = 1
            while shift < window:
                s = s + pltpu.roll(s, shift, axis=0)
                shift *= 2
            inv_cnt = 1.0 / jnp.minimum(pos + 1, window).astype(F32)
            p_ref[...] = (s[POOL_HALO:] * inv_cnt - u[POOL_HALO:]).astype(p_ref.dtype)

    mixed = jnp.dot(p_ref[...], wpg_ref[...], preferred_element_type=F32) * scale_ref[...]
    o_ref[...] = mixed.astype(o_ref.dtype)


def _pool_branch(h, w_in, w_pg, scale, batch, seq, tm=1024):
    n, d = h.shape
    groups, gw = w_pg.shape[0], w_pg.shape[1]
    tm = min(tm, seq)
    n_i = seq // tm
    halo_per_tm = tm // POOL_HALO

    def halo_map(b, i, g):
        return (jnp.maximum((b * n_i + i) * halo_per_tm - 1, 0), 0)

    return pl.pallas_call(
        functools.partial(_pool_kernel, tm=tm),
        grid=(batch, n_i, groups),
        in_specs=[pl.BlockSpec((tm, d), lambda b, i, g: (b * n_i + i, 0)),
                  pl.BlockSpec((POOL_HALO, d), halo_map),
                  pl.BlockSpec((d, gw), lambda b, i, g: (0, g)),
                  pl.BlockSpec((None, gw, gw), lambda b, i, g: (g, 0, 0)),
                  pl.BlockSpec((1, gw), lambda b, i, g: (0, g))],
        out_specs=pl.BlockSpec((tm, gw), lambda b, i, g: (b * n_i + i, g)),
        out_shape=jax.ShapeDtypeStruct((n, groups * gw), BF16),
        scratch_shapes=[pltpu.VMEM((POOL_HALO + tm, gw), F32), pltpu.VMEM((tm, gw), BF16)],
        compiler_params=_params("parallel", "parallel", "arbitrary"),
        name="pool_branch",
    )(h, h, w_in, w_pg, scale.reshape(1, -1))


def _mem_attn_kernel(h_ref, wq_ref, k_ref, v_ref, o_ref, *, scale):
    q = jnp.dot(h_ref[...], wq_ref[...], preferred_element_type=F32).astype(BF16)
    s = lax.dot_general(q, k_ref[...], (((1,), (1,)), ((), ())), preferred_element_type=F32) * scale
    m = jnp.max(s, axis=-1, keepdims=True)
    p = jnp.exp(s - m)
    inv_den = 1.0 / jnp.sum(p, axis=-1, keepdims=True)
    o = jnp.dot(p.astype(BF16), v_ref[...], preferred_element_type=F32) * inv_den
    o_ref[...] = o.astype(o_ref.dtype)


def _mem_attn(h, w_in, q_col0, mkv, batch, seq, mem_len, tm=1024):
    n, d = h.shape
    hd = mkv.shape[1] // (2 * MEM_HEADS)
    tm = min(tm, seq)
    n_i = seq // tm
    q_blk0 = q_col0 // hd
    assert q_col0 % hd == 0
    return pl.pallas_call(
        functools.partial(_mem_attn_kernel, scale=1.0 / math.sqrt(hd)),
        grid=(batch, n_i, MEM_HEADS),
        in_specs=[pl.BlockSpec((tm, d), lambda b, i, hh: (b * n_i + i, 0)),
                  pl.BlockSpec((d, hd), lambda b, i, hh: (0, q_blk0 + hh)),
                  pl.BlockSpec((mem_len, hd), lambda b, i, hh: (b, hh)),
                  pl.BlockSpec((mem_len, hd), lambda b, i, hh: (b, MEM_HEADS + hh))],
        out_specs=pl.BlockSpec((tm, hd), lambda b, i, hh: (b * n_i + i, hh)),
        out_shape=jax.ShapeDtypeStruct((n, MEM_HEADS * hd), BF16),
        compiler_params=_params("parallel", "parallel", "arbitrary"),
        name="mem_attn",
    )(h, w_in, mkv, mkv)


def _t5_bucket(dist):
    max_exact = NUM_BUCKETS // 2
    d32 = jnp.maximum(dist, 1).astype(F32)
    large = max_exact + (jnp.log(d32 / max_exact) / math.log(MAX_DISTANCE / max_exact)
                         * (NUM_BUCKETS - max_exact)).astype(jnp.int32)
    large = jnp.minimum(large, NUM_BUCKETS - 1)
    return jnp.where(dist < max_exact, dist, large)


def _bucket_tables():
    qi = jnp.arange(ATT_BLOCK)[:, None]
    kj = jnp.arange(2 * ATT_BLOCK)[None, :]
    delta = qi + ATT_BLOCK - kj
    band = (delta >= 0) & (delta <= ATT_BLOCK)
    tabs = [jnp.where(band, _t5_bucket(jnp.maximum(delta, 0) * dil), -1) for _, dil in DIL_CONFIGS]
    return jnp.stack(tabs).astype(jnp.int32)


def _dil_attn_kernel(rb_ref, bkt_ref, q0, q1, q2, k0, k1, k2, v0, v1, v2, o_ref,
                     og0, og1, og2, lg0, lg1, lg2, *, seq, heads, scale):
    head = pl.program_id(1)
    q_refs, k_refs, v_refs = (q0, q1, q2), (k0, k1, k2), (v0, v1, v2)
    o_nat, lse_nat = (og0, og1, og2), (lg0, lg1, lg2)
    blk = ATT_BLOCK

    for g, (_, dil) in enumerate(DIL_CONFIGS):
        sub_len = seq // dil
        n_blk = sub_len // blk
        bkt = bkt_ref[g]
        bias = jnp.full(bkt.shape, MASKED, F32)
        for b in range(NUM_BUCKETS):
            bias = jnp.where(bkt == b, rb_ref[b, g * heads + head], bias)

        for r in range(dil):
            rows = pl.ds(r, sub_len, stride=dil) if dil > 1 else pl.ds(0, sub_len)
            q_r = q_refs[g][rows, :].astype(BF16)
            k_r = k_refs[g][rows, :].astype(BF16)
            v_r = v_refs[g][rows, :].astype(BF16)
            for n in range(n_blk):
                lo = max(n - 1, 0) * blk
                qb = q_r[n * blk:(n + 1) * blk]
                kk = k_r[lo:(n + 1) * blk]
                vv = v_r[lo:(n + 1) * blk]
                tb = bias if n > 0 else bias[:, blk:]
                s = lax.dot_general(qb, kk, (((1,), (1,)), ((), ())),
                                    preferred_element_type=F32) * scale + tb
                m = jnp.max(s, axis=-1, keepdims=True)
                p = jnp.exp(s - m)
                den = jnp.sum(p, axis=-1, keepdims=True)
                o = jnp.dot(p.astype(BF16), vv, preferred_element_type=F32) * (1.0 / den)
                lse = m + jnp.log(den)
                start = r + dil * n * blk
                nat = pl.ds(start, blk, stride=dil) if dil > 1 else pl.ds(start, blk)
                o_nat[g][nat, :] = o
                lse_nat[g][nat, :] = jnp.broadcast_to(lse, (blk, HEAD_DIM))

    l0, l1, l2 = lse_nat[0][...], lse_nat[1][...], lse_nat[2][...]
    top = jnp.maximum(jnp.maximum(l0, l1), l2)
    w0, w1, w2 = jnp.exp(l0 - top), jnp.exp(l1 - top), jnp.exp(l2 - top)
    inv = 1.0 / (w0 + w1 + w2)
    out = (o_nat[0][...] * w0 + o_nat[1][...] * w1 + o_nat[2][...] * w2) * inv
    o_ref[...] = out.astype(o_ref.dtype)


def _dil_attn(q, kv, rel_bias, batch, seq):
    n = q.shape[0]
    n_groups = len(DIL_CONFIGS)
    heads = rel_bias.shape[1] // n_groups
    gh = n_groups * heads

    def col_spec(off):
        return pl.BlockSpec((seq, HEAD_DIM), lambda b, hh: (b, off + hh))

    in_specs = [pl.BlockSpec(memory_space=pltpu.SMEM),
                pl.BlockSpec((n_groups, ATT_BLOCK, 2 * ATT_BLOCK), lambda b, hh: (0, 0, 0))]
    in_specs += [col_spec(g * heads) for g in range(n_groups)]
    in_specs += [col_spec(g * heads) for g in range(n_groups)]
    in_specs += [col_spec(gh + g * heads) for g in range(n_groups)]
    return pl.pallas_call(
        functools.partial(_dil_attn_kernel, seq=seq, heads=heads, scale=1.0 / math.sqrt(HEAD_DIM)),
        grid=(batch, heads),
        in_specs=in_specs,
        out_specs=pl.BlockSpec((seq, HEAD_DIM), lambda b, hh: (b, hh)),
        out_shape=jax.ShapeDtypeStruct((n, heads * HEAD_DIM), BF16),
        scratch_shapes=[pltpu.VMEM((seq, HEAD_DIM), F32)] * (2 * n_groups),
        compiler_params=_params("parallel", "arbitrary"),
        name="dil_attn",
    )(rel_bias.astype(F32), _bucket_tables(), q, q, q, kv, kv, kv, kv, kv, kv)


def kernel(x, mem, a_norm, a_w_in, a_w_pg, a_scale, a_w_out, kv_norm, w_kv, b_norm, b_w_in,
           b_w_out, mem_norm, w_mem_kv, mlp_norm, mlp_w1, mlp_w2, rel_bias, final_norm):
    batch, seq, d = x.shape
    mem_len = mem.shape[1]
    n = batch * seq
    pool_width = a_scale.shape[-1]
    dil_q_width = w_kv.shape[1] // 2
    assert a_norm.shape[0] == 1 and b_norm.shape[0] == 1 and mlp_norm.shape[0] == 2

    def bf(w):
        return w.astype(BF16)

    def mlp(xs, layer):
        (hm,) = _rmsnorm(xs, [mlp_norm[layer]], BF16)
        hid = _linear([hm], bf(mlp_w1[layer]), mlp_w1.shape[2], BF16, act="relu2", name="mlp_up")
        return _linear([hid], bf(mlp_w2[layer]), d, F32, res=xs, tk=2048, name="mlp_down")

    x0 = x.reshape(n, d)
    (mem_h,) = _rmsnorm(mem.reshape(batch * mem_len, d), [mem_norm], BF16)
    mkv = [_linear([mem_h], bf(w_mem_kv[l]), w_mem_kv.shape[2], BF16, name="mem_kv")
           for l in range(2)]

    (h0,) = _rmsnorm(x0, [a_norm[0]], BF16)
    a_in = bf(a_w_in[0])
    mixed = _pool_branch(h0, a_in, bf(a_w_pg[0]), a_scale[0], batch, seq)
    mem_out = _mem_attn(h0, a_in, pool_width, mkv[0], batch, seq, mem_len)
    x1 = _linear([mixed, mem_out], bf(a_w_out[0]), d, F32, res=x0, tn=512, name="a_out")
    x2 = mlp(x1, 0)

    h_kv, h_b = _rmsnorm(x2, [kv_norm, b_norm[0]], BF16)
    kv = _linear([h_kv], bf(w_kv), w_kv.shape[1], F32, name="kv_proj")
    b_in = bf(b_w_in[0])
    q = _linear([h_b], b_in, dil_q_width, F32, name="q_proj")
    dil_out = _dil_attn(q, kv, rel_bias, batch, seq)
    mem_out = _mem_attn(h_b, b_in, dil_q_width, mkv[1], batch, seq, mem_len)
    x3 = _linear([dil_out, mem_out], bf(b_w_out[0]), d, F32, res=x2, name="b_out")
    x4 = mlp(x3, 1)

    (out,) = _rmsnorm(x4, [final_norm], F32)
    return out.reshape(batch, seq, d)
```

```python
import functools
import math

import jax
import jax.numpy as jnp
from jax import lax
from jax.experimental import pallas as pl
from jax.experimental.pallas import tpu as pltpu

F32 = jnp.float32
BF16 = jnp.bfloat16

EPS = 1e-6
POOL_WINDOWS = (2, 4, 8, 16)
POOL_HALO = 16
MEM_HEADS = 4
HEAD_DIM = 128
DIL_CONFIGS = ((128, 1), (512, 4), (2048, 16))
ATT_BLOCK = 128
NUM_BUCKETS = 32
MAX_DISTANCE = 2048
MASKED = -1e30

V7X_VMEM_BYTES = 64 * 1024 * 1024
VMEM_LIMIT_BYTES = V7X_VMEM_BYTES - 8 * 1024 * 1024


def _params(*semantics):
    return pltpu.CompilerParams(dimension_semantics=semantics, vmem_limit_bytes=VMEM_LIMIT_BYTES)


def _rmsnorm_kernel(x_ref, g_ref, *o_refs):
    x = x_ref[...]
    y = x * lax.rsqrt(jnp.mean(x * x, axis=-1, keepdims=True) + EPS)
    for n, o_ref in enumerate(o_refs):
        o_ref[...] = (y * g_ref[n:n + 1, :]).astype(o_ref.dtype)


def _rmsnorm(x, gains, out_dtype, tm=256):
    m, d = x.shape
    n = len(gains)
    g = jnp.stack(gains).astype(F32)
    outs = pl.pallas_call(
        _rmsnorm_kernel,
        grid=(m // tm,),
        in_specs=[pl.BlockSpec((tm, d), lambda i: (i, 0)),
                  pl.BlockSpec((n, d), lambda i: (0, 0))],
        out_specs=[pl.BlockSpec((tm, d), lambda i: (i, 0))] * n,
        out_shape=[jax.ShapeDtypeStruct((m, d), out_dtype)] * n,
        compiler_params=_params("parallel"),
        name="rmsnorm",
    )(x, g)
    return list(outs)


def _linear_kernel(*refs, k_sizes, nk, act, has_res, has_side):
    refs = list(refs)
    lhs_refs = [refs.pop(0) for _ in k_sizes]
    w_ref = refs.pop(0)
    res_ref = refs.pop(0) if has_res else None
    side_src_ref = refs.pop(0) if has_side else None
    o_ref = refs.pop(0)
    side_dst_ref = refs.pop(0) if has_side else None
    acc_ref = refs.pop(0) if nk > 1 else None

    if has_side:
        side_dst_ref[...] = side_src_ref[...].astype(side_dst_ref.dtype)

    def product():
        y, off = None, 0
        for lhs_ref, ks in zip(lhs_refs, k_sizes):
            w = w_ref[off:off + ks, :].astype(BF16)
            t = jnp.dot(lhs_ref[...], w, preferred_element_type=F32)
            y = t if y is None else y + t
            off += ks
        return y

    def finish(y):
        if act == "relu2":
            y = jnp.maximum(y, 0.0)
            y = y * y
        if has_res:
            y = res_ref[...] + y
        o_ref[...] = y.astype(o_ref.dtype)

    if nk == 1:
        finish(product())
        return

    k = pl.program_id(2)

    @pl.when(k == 0)
    def _():
        acc_ref[...] = jnp.zeros_like(acc_ref)

    acc_ref[...] += product()

    @pl.when(k == nk - 1)
    def _():
        finish(acc_ref[...])


def _linear(lhs_list, w, n_out, out_dtype, *, res=None, act=None, tm=1024, tn=1024, tk=4096,
            single_buffer_lhs=False, side_cast=None, name="linear"):
    m = lhs_list[0].shape[0]
    k_sizes = [a.shape[1] for a in lhs_list]
    k_total = sum(k_sizes)
    w_layer = 0
    if isinstance(w, tuple):
        w_layer, w = w
        w = w.reshape(-1, w.shape[-1])
    assert w.shape[0] % k_total == 0
    tm, tn = math.gcd(tm, m), math.gcd(tn, n_out)
    if len(lhs_list) > 1:
        assert k_total <= tk
        tk = k_total
    else:
        tk = math.gcd(tk, k_total)
        k_sizes = [tk]
    nk = k_total // tk
    n_i, n_j = m // tm, n_out // tn

    lhs_mode = dict(pipeline_mode=pl.Buffered(1)) if single_buffer_lhs else {}
    in_specs = [pl.BlockSpec((tm, ks), lambda i, j, k: (i, k), **lhs_mode) for ks in k_sizes]
    in_specs.append(pl.BlockSpec((tk, tn), lambda i, j, k: (w_layer * nk + k, j)))
    args = list(lhs_list) + [w]
    if res is not None:
        in_specs.append(pl.BlockSpec((tm, tn), lambda i, j, k: (i, j)))
        args.append(res)
    out_specs = [pl.BlockSpec((tm, tn), lambda i, j, k: (i, j))]
    out_shape = [jax.ShapeDtypeStruct((m, n_out), out_dtype)]
    if side_cast is not None:
        side_layer, side_stack = side_cast
        _, rows, cols = side_stack.shape
        steps = n_i * n_j * nk
        slab = rows // steps
        assert slab * steps == rows and slab % 16 == 0
        in_specs.append(pl.BlockSpec(
            (slab, cols), lambda i, j, k: (side_layer * steps + (i * n_j + j) * nk + k, 0)))
        args.append(side_stack.reshape(-1, cols))
        out_specs.append(pl.BlockSpec((slab, cols), lambda i, j, k: ((i * n_j + j) * nk + k, 0)))
        out_shape.append(jax.ShapeDtypeStruct((rows, cols), BF16))
    kernel = functools.partial(_linear_kernel, k_sizes=tuple(k_sizes), nk=nk, act=act,
                               has_res=res is not None, has_side=side_cast is not None)
    outs = pl.pallas_call(
        kernel,
        grid=(n_i, n_j, nk),
        in_specs=in_specs,
        out_specs=out_specs,
        out_shape=out_shape,
        scratch_shapes=[pltpu.VMEM((tm, tn), F32)] if nk > 1 else [],
        compiler_params=_params("parallel", "parallel", "arbitrary"),
        name=name,
    )(*args)
    return outs[0] if side_cast is None else tuple(outs)


def _pool_kernel(h_ref, halo_ref, w_ref, wpg_ref, scale_ref, o_ref, u_ref, p_ref, *, tm):
    i = pl.program_id(1)
    g = pl.program_id(2)
    w = w_ref[...]
    u_ref[POOL_HALO:, :] = jnp.dot(h_ref[...], w, preferred_element_type=F32)
    u_halo = jnp.dot(halo_ref[...], w, preferred_element_type=F32)
    u_ref[:POOL_HALO, :] = jnp.where(i > 0, u_halo, 0.0)
    pos = i * tm + lax.broadcasted_iota(jnp.int32, (tm, 1), 0)

    for group, window in enumerate(POOL_WINDOWS):
        @pl.when(g == group)
        def _(window=window):
            u = u_ref[...]
            s = u
            shift = 1
            while shift < window:
                s = s + pltpu.roll(s, shift, axis=0)
                shift *= 2
            inv_cnt = 1.0 / jnp.minimum(pos + 1, window).astype(F32)
            p_ref[...] = (s[POOL_HALO:] * inv_cnt - u[POOL_HALO:]).astype(p_ref.dtype)

    mixed = jnp.dot(p_ref[...], wpg_ref[...], preferred_element_type=F32) * scale_ref[...]
    o_ref[...] = mixed.astype(o_ref.dtype)


def _pool_branch(h, w_in, w_pg, scale, batch, seq, tm=1024):
    n, d = h.shape
    groups, gw = w_pg.shape[0], w_pg.shape[1]
    tm = min(tm, seq)
    n_i = seq // tm
    halo_per_tm = tm // POOL_HALO

    def halo_map(b, i, g):
        return (jnp.maximum((b * n_i + i) * halo_per_tm - 1, 0), 0)

    return pl.pallas_call(
        functools.partial(_pool_kernel, tm=tm),
        grid=(batch, n_i, groups),
        in_specs=[pl.BlockSpec((tm, d), lambda b, i, g: (b * n_i + i, 0)),
                  pl.BlockSpec((POOL_HALO, d), halo_map),
                  pl.BlockSpec((d, gw), lambda b, i, g: (0, g)),
                  pl.BlockSpec((None, gw, gw), lambda b, i, g: (g, 0, 0)),
                  pl.BlockSpec((1, gw), lambda b, i, g: (0, g))],
        out_specs=pl.BlockSpec((tm, gw), lambda b, i, g: (b * n_i + i, g)),
        out_shape=jax.ShapeDtypeStruct((n, groups * gw), BF16),
        scratch_shapes=[pltpu.VMEM((POOL_HALO + tm, gw), F32), pltpu.VMEM((tm, gw), BF16)],
        compiler_params=_params("parallel", "parallel", "arbitrary"),
        name="pool_branch",
    )(h, h, w_in, w_pg, scale.reshape(1, -1))


def _mem_attn_kernel(h_ref, wq_ref, k_ref, v_ref, o_ref, *, scale):
    wq = wq_ref[...].astype(BF16)
    q = jnp.dot(h_ref[...], wq, preferred_element_type=F32).astype(BF16)
    s = lax.dot_general(q, k_ref[...], (((1,), (1,)), ((), ())), preferred_element_type=F32) * scale
    m = jnp.max(s, axis=-1, keepdims=True)
    p = jnp.exp(s - m)
    inv_den = 1.0 / jnp.sum(p, axis=-1, keepdims=True)
    o = jnp.dot(p.astype(BF16), v_ref[...], preferred_element_type=F32) * inv_den
    o_ref[...] = o.astype(o_ref.dtype)


def _mem_attn(h, w_in, q_col0, mkv, batch, seq, mem_len, tm=1024):
    n, d = h.shape
    hd = mkv.shape[1] // (2 * MEM_HEADS)
    tm = min(tm, seq)
    n_i = seq // tm
    q_blk0 = q_col0 // hd
    assert q_col0 % hd == 0
    return pl.pallas_call(
        functools.partial(_mem_attn_kernel, scale=1.0 / math.sqrt(hd)),
        grid=(batch, n_i, MEM_HEADS),
        in_specs=[pl.BlockSpec((tm, d), lambda b, i, hh: (b * n_i + i, 0)),
                  pl.BlockSpec((d, hd), lambda b, i, hh: (0, q_blk0 + hh)),
                  pl.BlockSpec((mem_len, hd), lambda b, i, hh: (b, hh)),
                  pl.BlockSpec((mem_len, hd), lambda b, i, hh: (b, MEM_HEADS + hh))],
        out_specs=pl.BlockSpec((tm, hd), lambda b, i, hh: (b * n_i + i, hh)),
        out_shape=jax.ShapeDtypeStruct((n, MEM_HEADS * hd), BF16),
        compiler_params=_params("parallel", "parallel", "arbitrary"),
        name="mem_attn",
    )(h, w_in, mkv, mkv)


def _t5_bucket(dist):
    max_exact = NUM_BUCKETS // 2
    d32 = jnp.maximum(dist, 1).astype(F32)
    large = max_exact + (jnp.log(d32 / max_exact) / math.log(MAX_DISTANCE / max_exact)
                         * (NUM_BUCKETS - max_exact)).astype(jnp.int32)
    large = jnp.minimum(large, NUM_BUCKETS - 1)
    return jnp.where(dist < max_exact, dist, large)


def _bucket_tables():
    qi = jnp.arange(ATT_BLOCK)[:, None]
    kj = jnp.arange(2 * ATT_BLOCK)[None, :]
    delta = qi + ATT_BLOCK - kj
    band = (delta >= 0) & (delta <= ATT_BLOCK)
    tabs = [jnp.where(band, _t5_bucket(jnp.maximum(delta, 0) * dil), -1) for _, dil in DIL_CONFIGS]
    return jnp.stack(tabs).astype(jnp.int32)


def _dil_attn_kernel(rb_ref, bkt_ref, q0, q1, q2, k0, k1, k2, v0, v1, v2, o_ref,
                     og0, og1, og2, lg0, lg1, lg2, *, seq, heads, scale):
    head = pl.program_id(1)
    q_refs, k_refs, v_refs = (q0, q1, q2), (k0, k1, k2), (v0, v1, v2)
    o_nat, lse_nat = (og0, og1, og2), (lg0, lg1, lg2)
    blk = ATT_BLOCK

    for g, (_, dil) in enumerate(DIL_CONFIGS):
        sub_len = seq // dil
        n_blk = sub_len // blk
        bkt = bkt_ref[g]
        bias = jnp.full(bkt.shape, MASKED, F32)
        for b in range(NUM_BUCKETS):
            bias = jnp.where(bkt == b, rb_ref[b, g * heads + head], bias)

        for r in range(dil):
            rows = pl.ds(r, sub_len, stride=dil) if dil > 1 else pl.ds(0, sub_len)
            q_r = q_refs[g][rows, :].astype(BF16)
            k_r = k_refs[g][rows, :].astype(BF16)
            v_r = v_refs[g][rows, :].astype(BF16)
            for n in range(n_blk):
                lo = max(n - 1, 0) * blk
                qb = q_r[n * blk:(n + 1) * blk]
                kk = k_r[lo:(n + 1) * blk]
                vv = v_r[lo:(n + 1) * blk]
                tb = bias if n > 0 else bias[:, blk:]
                s = lax.dot_general(qb, kk, (((1,), (1,)), ((), ())),
                                    preferred_element_type=F32) * scale + tb
                m = jnp.max(s, axis=-1, keepdims=True)
                p = jnp.exp(s - m)
                den = jnp.sum(p, axis=-1, keepdims=True)
                o = jnp.dot(p.astype(BF16), vv, preferred_element_type=F32) * (1.0 / den)
                lse = m + jnp.log(den)
                start = r + dil * n * blk
                nat = pl.ds(start, blk, stride=dil) if dil > 1 else pl.ds(start, blk)
                o_nat[g][nat, :] = o
                lse_nat[g][nat, :] = jnp.broadcast_to(lse, (blk, HEAD_DIM))

    l0, l1, l2 = lse_nat[0][...], lse_nat[1][...], lse_nat[2][...]
    top = jnp.maximum(jnp.maximum(l0, l1), l2)
    w0, w1, w2 = jnp.exp(l0 - top), jnp.exp(l1 - top), jnp.exp(l2 - top)
    inv = 1.0 / (w0 + w1 + w2)
    out = (o_nat[0][...] * w0 + o_nat[1][...] * w1 + o_nat[2][...] * w2) * inv
    o_ref[...] = out.astype(o_ref.dtype)


def _dil_attn(q, kv, rel_bias, batch, seq):
    n = q.shape[0]
    n_groups = len(DIL_CONFIGS)
    heads = rel_bias.shape[1] // n_groups
    gh = n_groups * heads

    def col_spec(off):
        return pl.BlockSpec((seq, HEAD_DIM), lambda b, hh: (b, off + hh))

    in_specs = [pl.BlockSpec(memory_space=pltpu.SMEM),
                pl.BlockSpec((n_groups, ATT_BLOCK, 2 * ATT_BLOCK), lambda b, hh: (0, 0, 0))]
    in_specs += [col_spec(g * heads) for g in range(n_groups)]
    in_specs += [col_spec(g * heads) for g in range(n_groups)]
    in_specs += [col_spec(gh + g * heads) for g in range(n_groups)]
    return pl.pallas_call(
        functools.partial(_dil_attn_kernel, seq=seq, heads=heads, scale=1.0 / math.sqrt(HEAD_DIM)),
        grid=(batch, heads),
        in_specs=in_specs,
        out_specs=pl.BlockSpec((seq, HEAD_DIM), lambda b, hh: (b, hh)),
        out_shape=jax.ShapeDtypeStruct((n, heads * HEAD_DIM), BF16),
        scratch_shapes=[pltpu.VMEM((seq, HEAD_DIM), F32)] * (2 * n_groups),
        compiler_params=_params("parallel", "arbitrary"),
        name="dil_attn",
    )(rel_bias.astype(F32), _bucket_tables(), q, q, q, kv, kv, kv, kv, kv, kv)


def kernel(x, mem, a_norm, a_w_in, a_w_pg, a_scale, a_w_out, kv_norm, w_kv, b_norm, b_w_in,
           b_w_out, mem_norm, w_mem_kv, mlp_norm, mlp_w1, mlp_w2, rel_bias, final_norm):
    batch, seq, d = x.shape
    mem_len = mem.shape[1]
    n = batch * seq
    pool_width = a_scale.shape[-1]
    dil_q_width = w_kv.shape[1] // 2
    assert a_norm.shape[0] == 1 and b_norm.shape[0] == 1 and mlp_norm.shape[0] == 2

    def mlp(xs, layer):
        (hm,) = _rmsnorm(xs, [mlp_norm[layer]], BF16)
        hid, w2 = _linear([hm], (layer, mlp_w1), mlp_w1.shape[2], BF16, act="relu2", tm=2048,
                          tn=512, single_buffer_lhs=True, side_cast=(layer, mlp_w2), name="mlp_up")
        return _linear([hid], w2, d, F32, res=xs, tk=2048, name="mlp_down")

    x0 = x.reshape(n, d)
    (mem_h,) = _rmsnorm(mem.reshape(batch * mem_len, d), [mem_norm], BF16)
    mkv = [_linear([mem_h], (l, w_mem_kv), w_mem_kv.shape[2], BF16, tm=2048, tn=512,
                   single_buffer_lhs=True, name="mem_kv") for l in range(2)]

    (h0,) = _rmsnorm(x0, [a_norm[0]], BF16)
    mixed = _pool_branch(h0, a_w_in[0, :, :pool_width].astype(BF16), a_w_pg[0].astype(BF16),
                         a_scale[0], batch, seq)
    mem_out = _mem_attn(h0, a_w_in[0], pool_width, mkv[0], batch, seq, mem_len)
    x1 = _linear([mixed, mem_out], a_w_out[0], d, F32, res=x0, tn=512, name="a_out")
    x2 = mlp(x1, 0)

    h_kv, h_b = _rmsnorm(x2, [kv_norm, b_norm[0]], BF16)
    kv = _linear([h_kv], w_kv, w_kv.shape[1], F32, tm=2048, tn=512, single_buffer_lhs=True,
                 name="kv_proj")
    q = _linear([h_b], b_w_in[0], dil_q_width, F32, tn=512, name="q_proj")
    dil_out = _dil_attn(q, kv, rel_bias, batch, seq)
    mem_out = _mem_attn(h_b, b_w_in[0], dil_q_width, mkv[1], batch, seq, mem_len)
    x3 = _linear([dil_out, mem_out], b_w_out[0], d, F32, res=x2, name="b_out")
    x4 = mlp(x3, 1)

    (out,) = _rmsnorm(x4, [final_norm], F32)
    return out.reshape(batch, seq, d)
```

```python
import functools
import math

import jax
import jax.numpy as jnp
from jax import lax
from jax.experimental import pallas as pl
from jax.experimental.pallas import tpu as pltpu

F32 = jnp.float32
BF16 = jnp.bfloat16

EPS = 1e-6
POOL_WINDOWS = (2, 4, 8, 16)
POOL_HALO = 16
MEM_HEADS = 4
HEAD_DIM = 128
DIL_CONFIGS = ((128, 1), (512, 4), (2048, 16))
ATT_BLOCK = 128
NUM_BUCKETS = 32
MAX_DISTANCE = 2048
MASKED = -1e30

V7X_VMEM_BYTES = 64 * 1024 * 1024
VMEM_LIMIT_BYTES = V7X_VMEM_BYTES - 4 * 1024 * 1024


def _params(*semantics):
    return pltpu.CompilerParams(dimension_semantics=semantics, vmem_limit_bytes=VMEM_LIMIT_BYTES)


def _rmsnorm_kernel(x_ref, g_ref, *o_refs):
    x = x_ref[...]
    y = x * lax.rsqrt(jnp.mean(x * x, axis=-1, keepdims=True) + EPS)
    for n, o_ref in enumerate(o_refs):
        o_ref[...] = (y * g_ref[n:n + 1, :]).astype(o_ref.dtype)


def _rmsnorm(x, gains, out_dtype, tm=256):
    m, d = x.shape
    n = len(gains)
    g = jnp.stack(gains).astype(F32)
    outs = pl.pallas_call(
        _rmsnorm_kernel,
        grid=(m // tm,),
        in_specs=[pl.BlockSpec((tm, d), lambda i: (i, 0)),
                  pl.BlockSpec((n, d), lambda i: (0, 0))],
        out_specs=[pl.BlockSpec((tm, d), lambda i: (i, 0))] * n,
        out_shape=[jax.ShapeDtypeStruct((m, d), out_dtype)] * n,
        compiler_params=_params("parallel"),
        name="rmsnorm",
    )(x, g)
    return list(outs)


def _side_cast_plan(side_casts, steps, step_index):
    in_specs, args, out_specs, out_shapes, shapes = [], [], [], [], []
    for layer, stack in side_casts:
        _, rows, cols = stack.shape
        fold = 1
        while (rows * fold) % (steps * 16):
            fold *= 2
        assert cols % (fold * 128) == 0
        r, c = rows * fold, cols // fold
        slab = r // steps
        in_specs.append(pl.BlockSpec(
            (slab, c), lambda *g, layer=layer: (layer * steps + step_index(*g), 0)))
        args.append(stack.reshape(-1, c))
        out_specs.append(pl.BlockSpec((slab, c), lambda *g: (step_index(*g), 0)))
        out_shapes.append(jax.ShapeDtypeStruct((r, c), BF16))
        shapes.append((rows, cols))

    def finalize(outs):
        return [o.reshape(s) for o, s in zip(outs, shapes)]

    return in_specs, args, out_specs, out_shapes, finalize


def _run_side_casts(src_refs, dst_refs):
    for src_ref, dst_ref in zip(src_refs, dst_refs):
        dst_ref[...] = src_ref[...].astype(dst_ref.dtype)


def _linear_kernel(*refs, k_sizes, nk, act, has_res, n_side):
    refs = list(refs)
    lhs_refs = [refs.pop(0) for _ in k_sizes]
    w_ref = refs.pop(0)
    res_ref = refs.pop(0) if has_res else None
    side_src = [refs.pop(0) for _ in range(n_side)]
    o_ref = refs.pop(0)
    side_dst = [refs.pop(0) for _ in range(n_side)]
    _run_side_casts(side_src, side_dst)

    def product():
        y, off = None, 0
        for lhs_ref, ks in zip(lhs_refs, k_sizes):
            w = w_ref[off:off + ks, :].astype(BF16)
            t = jnp.dot(lhs_ref[...], w, preferred_element_type=F32)
            y = t if y is None else y + t
            off += ks
        return y

    if nk == 1:
        y = product()
        if act == "relu2":
            y = jnp.maximum(y, 0.0)
            y = y * y
        if has_res:
            y = res_ref[...] + y
        o_ref[...] = y.astype(o_ref.dtype)
        return

    k = pl.program_id(2)

    @pl.when(k == 0)
    def _():
        o_ref[...] = res_ref[...] if has_res else jnp.zeros_like(o_ref)

    o_ref[...] += product()


def _linear(lhs_list, w, n_out, out_dtype, *, res=None, act=None, tm=1024, tn=1024, tk=4096,
            single_buffer_lhs=False, side_casts=(), name="linear"):
    m = lhs_list[0].shape[0]
    k_sizes = [a.shape[1] for a in lhs_list]
    k_total = sum(k_sizes)
    w_layer = 0
    if isinstance(w, tuple):
        w_layer, w = w
        w = w.reshape(-1, w.shape[-1])
    assert w.shape[0] % k_total == 0
    tm, tn = math.gcd(tm, m), math.gcd(tn, n_out)
    if len(lhs_list) > 1:
        assert k_total <= tk
        tk = k_total
    else:
        tk = math.gcd(tk, k_total)
        k_sizes = [tk]
    nk = k_total // tk
    assert nk == 1 or (act is None and out_dtype == F32)
    n_i, n_j = m // tm, n_out // tn

    lhs_mode = dict(pipeline_mode=pl.Buffered(1)) if single_buffer_lhs else {}
    in_specs = [pl.BlockSpec((tm, ks), lambda i, j, k: (i, k), **lhs_mode) for ks in k_sizes]
    in_specs.append(pl.BlockSpec((tk, tn), lambda i, j, k: (w_layer * nk + k, j)))
    args = list(lhs_list) + [w]
    if res is not None:
        in_specs.append(pl.BlockSpec((tm, tn), lambda i, j, k: (i, j)))
        args.append(res)
    side_in, side_args, side_out, side_shapes, finalize = _side_cast_plan(
        side_casts, n_i * n_j * nk, lambda i, j, k: (i * n_j + j) * nk + k)
    kernel = functools.partial(_linear_kernel, k_sizes=tuple(k_sizes), nk=nk, act=act,
                               has_res=res is not None, n_side=len(side_casts))
    outs = pl.pallas_call(
        kernel,
        grid=(n_i, n_j, nk),
        in_specs=in_specs + side_in,
        out_specs=[pl.BlockSpec((tm, tn), lambda i, j, k: (i, j))] + side_out,
        out_shape=[jax.ShapeDtypeStruct((m, n_out), out_dtype)] + side_shapes,
        compiler_params=_params("parallel", "parallel", "arbitrary"),
        name=name,
    )(*args, *side_args)
    return outs[0] if not side_casts else (outs[0], *finalize(outs[1:]))


def _pool_kernel(h_ref, halo_ref, w_ref, wpg_ref, scale_ref, *rest, tm, n_side):
    side_src, (o_ref, *side_dst), (u_ref, p_ref) = (
        rest[:n_side], rest[n_side:2 * n_side + 1], rest[2 * n_side + 1:])
    _run_side_casts(side_src, side_dst)
    i = pl.program_id(1)
    g = pl.program_id(2)
    w = w_ref[...]
    u_ref[POOL_HALO:, :] = jnp.dot(h_ref[...], w, preferred_element_type=F32)
    u_halo = jnp.dot(halo_ref[...], w, preferred_element_type=F32)
    u_ref[:POOL_HALO, :] = jnp.where(i > 0, u_halo, 0.0)
    pos = i * tm + lax.broadcasted_iota(jnp.int32, (tm, 1), 0)

    for group, window in enumerate(POOL_WINDOWS):
        @pl.when(g == group)
        def _(window=window):
            u = u_ref[...]
            s = u
            shift = 1
            while shift < window:
                s = s + pltpu.roll(s, shift, axis=0)
                shift *= 2
            inv_cnt = 1.0 / jnp.minimum(pos + 1, window).astype(F32)
            p_ref[...] = (s[POOL_HALO:] * inv_cnt - u[POOL_HALO:]).astype(p_ref.dtype)

    mixed = jnp.dot(p_ref[...], wpg_ref[...], preferred_element_type=F32) * scale_ref[...]
    o_ref[...] = mixed.astype(o_ref.dtype)


def _pool_branch(h, w_in, w_pg, scale, batch, seq, tm=1024, side_casts=()):
    n, d = h.shape
    groups, gw = w_pg.shape[0], w_pg.shape[1]
    tm = min(tm, seq)
    n_i = seq // tm
    halo_per_tm = tm // POOL_HALO

    def halo_map(b, i, g):
        return (jnp.maximum((b * n_i + i) * halo_per_tm - 1, 0), 0)

    side_in, side_args, side_out, side_shapes, finalize = _side_cast_plan(
        side_casts, batch * n_i * groups, lambda b, i, g: (b * n_i + i) * groups + g)
    outs = pl.pallas_call(
        functools.partial(_pool_kernel, tm=tm, n_side=len(side_casts)),
        grid=(batch, n_i, groups),
        in_specs=[pl.BlockSpec((tm, d), lambda b, i, g: (b * n_i + i, 0)),
                  pl.BlockSpec((POOL_HALO, d), halo_map),
                  pl.BlockSpec((d, gw), lambda b, i, g: (0, g)),
                  pl.BlockSpec((None, gw, gw), lambda b, i, g: (g, 0, 0)),
                  pl.BlockSpec((1, gw), lambda b, i, g: (0, g))] + side_in,
        out_specs=[pl.BlockSpec((tm, gw), lambda b, i, g: (b * n_i + i, g))] + side_out,
        out_shape=[jax.ShapeDtypeStruct((n, groups * gw), BF16)] + side_shapes,
        scratch_shapes=[pltpu.VMEM((POOL_HALO + tm, gw), F32), pltpu.VMEM((tm, gw), BF16)],
        compiler_params=_params("parallel", "parallel", "arbitrary"),
        name="pool_branch",
    )(h, h, w_in, w_pg, scale.reshape(1, -1), *side_args)
    return outs[0] if not side_casts else (outs[0], *finalize(outs[1:]))


def _mem_attn_kernel(h_ref, wq_ref, k_ref, v_ref, o_ref, *, scale):
    wq = wq_ref[...].astype(BF16)
    q = jnp.dot(h_ref[...], wq, preferred_element_type=F32).astype(BF16)
    s = lax.dot_general(q, k_ref[...], (((1,), (1,)), ((), ())), preferred_element_type=F32) * scale
    m = jnp.max(s, axis=-1, keepdims=True)
    p = jnp.exp(s - m)
    inv_den = 1.0 / jnp.sum(p, axis=-1, keepdims=True)
    o = jnp.dot(p.astype(BF16), v_ref[...], preferred_element_type=F32) * inv_den
    o_ref[...] = o.astype(o_ref.dtype)


def _mem_attn(h, w_in, q_col0, mkv, batch, seq, mem_len, tm=1024):
    n, d = h.shape
    hd = mkv.shape[1] // (2 * MEM_HEADS)
    tm = min(tm, seq)
    n_i = seq // tm
    q_blk0 = q_col0 // hd
    assert q_col0 % hd == 0
    return pl.pallas_call(
        functools.partial(_mem_attn_kernel, scale=1.0 / math.sqrt(hd)),
        grid=(batch, n_i, MEM_HEADS),
        in_specs=[pl.BlockSpec((tm, d), lambda b, i, hh: (b * n_i + i, 0)),
                  pl.BlockSpec((d, hd), lambda b, i, hh: (0, q_blk0 + hh)),
                  pl.BlockSpec((mem_len, hd), lambda b, i, hh: (b, hh)),
                  pl.BlockSpec((mem_len, hd), lambda b, i, hh: (b, MEM_HEADS + hh))],
        out_specs=pl.BlockSpec((tm, hd), lambda b, i, hh: (b * n_i + i, hh)),
        out_shape=jax.ShapeDtypeStruct((n, MEM_HEADS * hd), BF16),
        compiler_params=_params("parallel", "parallel", "arbitrary"),
        name="mem_attn",
    )(h, w_in, mkv, mkv)


def _t5_bucket(dist):
    max_exact = NUM_BUCKETS // 2
    d32 = jnp.maximum(dist, 1).astype(F32)
    large = max_exact + (jnp.log(d32 / max_exact) / math.log(MAX_DISTANCE / max_exact)
                         * (NUM_BUCKETS - max_exact)).astype(jnp.int32)
    large = jnp.minimum(large, NUM_BUCKETS - 1)
    return jnp.where(dist < max_exact, dist, large)


def _bucket_tables():
    qi = jnp.arange(ATT_BLOCK)[:, None]
    kj = jnp.arange(2 * ATT_BLOCK)[None, :]
    delta = qi + ATT_BLOCK - kj
    band = (delta >= 0) & (delta <= ATT_BLOCK)
    tabs = [jnp.where(band, _t5_bucket(jnp.maximum(delta, 0) * dil), -1) for _, dil in DIL_CONFIGS]
    return jnp.stack(tabs).astype(jnp.int32)


def _dil_attn_kernel(rb_ref, bkt_ref, q0, q1, q2, k0, k1, k2, v0, v1, v2, o_ref,
                     og0, og1, og2, lg0, lg1, lg2, *, seq, heads, scale):
    head = pl.program_id(1)
    q_refs, k_refs, v_refs = (q0, q1, q2), (k0, k1, k2), (v0, v1, v2)
    o_nat, lse_nat = (og0, og1, og2), (lg0, lg1, lg2)
    blk = ATT_BLOCK

    for g, (_, dil) in enumerate(DIL_CONFIGS):
        sub_len = seq // dil
        n_blk = sub_len // blk
        bkt = bkt_ref[g]
        bias = jnp.full(bkt.shape, MASKED, F32)
        for b in range(NUM_BUCKETS):
            bias = jnp.where(bkt == b, rb_ref[b, g * heads + head], bias)

        for r in range(dil):
            rows = pl.ds(r, sub_len, stride=dil) if dil > 1 else pl.ds(0, sub_len)
            q_r = q_refs[g][rows, :].astype(BF16)
            k_r = k_refs[g][rows, :].astype(BF16)
            v_r = v_refs[g][rows, :].astype(BF16)
            for n in range(n_blk):
                lo = max(n - 1, 0) * blk
                qb = q_r[n * blk:(n + 1) * blk]
                kk = k_r[lo:(n + 1) * blk]
                vv = v_r[lo:(n + 1) * blk]
                tb = bias if n > 0 else bias[:, blk:]
                s = lax.dot_general(qb, kk, (((1,), (1,)), ((), ())),
                                    preferred_element_type=F32) * scale + tb
                m = jnp.max(s, axis=-1, keepdims=True)
                p = jnp.exp(s - m)
                den = jnp.sum(p, axis=-1, keepdims=True)
                o = jnp.dot(p.astype(BF16), vv, preferred_element_type=F32) * (1.0 / den)
                lse = m + jnp.log(den)
                start = r + dil * n * blk
                nat = pl.ds(start, blk, stride=dil) if dil > 1 else pl.ds(start, blk)
                o_nat[g][nat, :] = o
                lse_nat[g][nat, :] = jnp.broadcast_to(lse, (blk, HEAD_DIM))

    l0, l1, l2 = lse_nat[0][...], lse_nat[1][...], lse_nat[2][...]
    top = jnp.maximum(jnp.maximum(l0, l1), l2)
    w0, w1, w2 = jnp.exp(l0 - top), jnp.exp(l1 - top), jnp.exp(l2 - top)
    inv = 1.0 / (w0 + w1 + w2)
    out = (o_nat[0][...] * w0 + o_nat[1][...] * w1 + o_nat[2][...] * w2) * inv
    o_ref[...] = out.astype(o_ref.dtype)


def _dil_attn(q, kv, rel_bias, batch, seq):
    n = q.shape[0]
    n_groups = len(DIL_CONFIGS)
    heads = rel_bias.shape[1] // n_groups
    gh = n_groups * heads

    def col_spec(off):
        return pl.BlockSpec((seq, HEAD_DIM), lambda b, hh: (b, off + hh))

    in_specs = [pl.BlockSpec(memory_space=pltpu.SMEM),
                pl.BlockSpec((n_groups, ATT_BLOCK, 2 * ATT_BLOCK), lambda b, hh: (0, 0, 0))]
    in_specs += [col_spec(g * heads) for g in range(n_groups)]
    in_specs += [col_spec(g * heads) for g in range(n_groups)]
    in_specs += [col_spec(gh + g * heads) for g in range(n_groups)]
    return pl.pallas_call(
        functools.partial(_dil_attn_kernel, seq=seq, heads=heads, scale=1.0 / math.sqrt(HEAD_DIM)),
        grid=(batch, heads),
        in_specs=in_specs,
        out_specs=pl.BlockSpec((seq, HEAD_DIM), lambda b, hh: (b, hh)),
        out_shape=jax.ShapeDtypeStruct((n, heads * HEAD_DIM), BF16),
        scratch_shapes=[pltpu.VMEM((seq, HEAD_DIM), F32)] * (2 * n_groups),
        compiler_params=_params("parallel", "arbitrary"),
        name="dil_attn",
    )(rel_bias.astype(F32), _bucket_tables(), q, q, q, kv, kv, kv, kv, kv, kv)


def kernel(x, mem, a_norm, a_w_in, a_w_pg, a_scale, a_w_out, kv_norm, w_kv, b_norm, b_w_in,
           b_w_out, mem_norm, w_mem_kv, mlp_norm, mlp_w1, mlp_w2, rel_bias, final_norm):
    batch, seq, d = x.shape
    mem_len = mem.shape[1]
    n = batch * seq
    pool_width = a_scale.shape[-1]
    dil_q_width = w_kv.shape[1] // 2
    assert a_norm.shape[0] == 1 and b_norm.shape[0] == 1 and mlp_norm.shape[0] == 2

    x0 = x.reshape(n, d)
    (mem_h,) = _rmsnorm(mem.reshape(batch * mem_len, d), [mem_norm], BF16)
    mkv = [_linear([mem_h], (l, w_mem_kv), w_mem_kv.shape[2], BF16, tm=2048, tn=512,
                   single_buffer_lhs=True, name="mem_kv") for l in range(2)]

    (h0,) = _rmsnorm(x0, [a_norm[0]], BF16)
    mixed, a_out_w = _pool_branch(h0, a_w_in[0, :, :pool_width].astype(BF16),
                                  a_w_pg[0].astype(BF16), a_scale[0], batch, seq,
                                  side_casts=[(0, a_w_out)])
    mem_out = _mem_attn(h0, a_w_in[0], pool_width, mkv[0], batch, seq, mem_len)
    x1 = _linear([mixed, mem_out], a_out_w, d, F32, res=x0, tn=512, name="a_out")
    (hm,) = _rmsnorm(x1, [mlp_norm[0]], BF16)
    hid, w2, w1_next = _linear([hm], (0, mlp_w1), mlp_w1.shape[2], BF16, act="relu2", tm=2048,
                               tn=512, single_buffer_lhs=True,
                               side_casts=[(0, mlp_w2), (1, mlp_w1)], name="mlp_up")
    x2, kv_w, b_in, b_out_w = _linear([hid], w2, d, F32, res=x1,
                                      side_casts=[(0, w_kv[None]), (0, b_w_in), (0, b_w_out)],
                                      name="mlp_down")

    h_kv, h_b = _rmsnorm(x2, [kv_norm, b_norm[0]], BF16)
    kv = _linear([h_kv], kv_w, w_kv.shape[1], F32, name="kv_proj")
    q = _linear([h_b], b_in, dil_q_width, F32, name="q_proj")
    dil_out = _dil_attn(q, kv, rel_bias, batch, seq)
    mem_out = _mem_attn(h_b, b_in, dil_q_width, mkv[1], batch, seq, mem_len)
    x3 = _linear([dil_out, mem_out], b_out_w, d, F32, res=x2, name="b_out")
    (hm,) = _rmsnorm(x3, [mlp_norm[1]], BF16)
    hid, w2 = _linear([hm], w1_next, mlp_w1.shape[2], BF16, act="relu2",
                      side_casts=[(1, mlp_w2)], name="mlp_up")
    x4 = _linear([hid], w2, d, F32, res=x3, name="mlp_down")

    (out,) = _rmsnorm(x4, [final_norm], F32)
    return out.reshape(batch, seq, d)
```

```python
import functools
import math

import jax
import jax.numpy as jnp
from jax import lax
from jax.experimental import pallas as pl
from jax.experimental.pallas import tpu as pltpu

F32 = jnp.float32
BF16 = jnp.bfloat16

EPS = 1e-6
POOL_WINDOWS = (2, 4, 8, 16)
POOL_HALO = 16
MEM_HEADS = 4
LANES = 128
HEAD_DIM = 128
DIL_CONFIGS = ((128, 1), (512, 4), (2048, 16))
ATT_BLOCK = 128
NUM_BUCKETS = 32
MAX_DISTANCE = 2048
MASKED = -1e30

V7X_VMEM_BYTES = 64 * 1024 * 1024
VMEM_LIMIT_BYTES = V7X_VMEM_BYTES - 4 * 1024 * 1024


def _params(*semantics):
    return pltpu.CompilerParams(dimension_semantics=semantics, vmem_limit_bytes=VMEM_LIMIT_BYTES)


def _rmsnorm_kernel(x_ref, g_ref, *o_refs):
    x = x_ref[...]
    y = x * lax.rsqrt(jnp.mean(x * x, axis=-1, keepdims=True) + EPS)
    for n, o_ref in enumerate(o_refs):
        o_ref[...] = (y * g_ref[n:n + 1, :]).astype(o_ref.dtype)


def _rmsnorm(x, gains, out_dtype, tm=512):
    m, d = x.shape
    n = len(gains)
    tm = math.gcd(tm, m)
    g = jnp.stack(gains).astype(F32)
    outs = pl.pallas_call(
        _rmsnorm_kernel,
        grid=(m // tm,),
        in_specs=[pl.BlockSpec((tm, d), lambda i: (i, 0)),
                  pl.BlockSpec((n, d), lambda i: (0, 0))],
        out_specs=[pl.BlockSpec((tm, d), lambda i: (i, 0))] * n,
        out_shape=[jax.ShapeDtypeStruct((m, d), out_dtype)] * n,
        compiler_params=_params("parallel"),
        name="rmsnorm",
    )(x, g)
    return list(outs)


def _side_cast_plan(side_casts, steps, step_index):
    in_specs, args, out_specs, out_shapes, shapes = [], [], [], [], []
    for layer, stack in side_casts:
        _, rows, cols = stack.shape
        fold = 1
        while (rows * fold) % (steps * 16):
            fold *= 2
        assert cols % (fold * 128) == 0
        r, c = rows * fold, cols // fold
        slab = r // steps
        in_specs.append(pl.BlockSpec(
            (slab, c), lambda *g, layer=layer: (layer * steps + step_index(*g), 0)))
        args.append(stack.reshape(-1, c))
        out_specs.append(pl.BlockSpec((slab, c), lambda *g: (step_index(*g), 0)))
        out_shapes.append(jax.ShapeDtypeStruct((r, c), BF16))
        shapes.append((rows, cols))

    def finalize(outs):
        return [o.reshape(s) for o, s in zip(outs, shapes)]

    return in_specs, args, out_specs, out_shapes, finalize


def _run_side_casts(src_refs, dst_refs):
    for src_ref, dst_ref in zip(src_refs, dst_refs):
        dst_ref[...] = src_ref[...].astype(dst_ref.dtype)


def _inv_rms(ss_ref, dim):
    return lax.rsqrt(ss_ref[:, 0:1] * (1.0 / dim) + EPS)


def _linear_kernel(*refs, k_sizes, nk, act, has_res, ss_dim, n_gain, n_side):
    refs = list(refs)
    lhs_refs = [refs.pop(0) for _ in k_sizes]
    w_ref = refs.pop(0)
    res_ref = refs.pop(0) if has_res else None
    in_ss_ref = refs.pop(0) if ss_dim else None
    gain_ref = refs.pop(0) if n_gain else None
    side_src = [refs.pop(0) for _ in range(n_side)]
    o_ref = refs.pop(0)
    scaled_refs = [refs.pop(0) for _ in range(n_gain)]
    out_ss_ref = refs.pop(0) if n_gain else None
    side_dst = [refs.pop(0) for _ in range(n_side)]
    _run_side_casts(side_src, side_dst)
    j = pl.program_id(1)

    def product():
        y, off = None, 0
        for lhs_ref, ks in zip(lhs_refs, k_sizes):
            w = w_ref[off:off + ks, :].astype(BF16)
            t = jnp.dot(lhs_ref[...], w, preferred_element_type=F32)
            y = t if y is None else y + t
            off += ks
        return y

    def emit_norm_inputs(y):
        for n, s_ref in enumerate(scaled_refs):
            s_ref[...] = (y * gain_ref[n:n + 1, :]).astype(s_ref.dtype)
        part = jnp.broadcast_to(jnp.sum(y * y, axis=-1, keepdims=True), out_ss_ref.shape)

        @pl.when(j == 0)
        def _():
            out_ss_ref[...] = part

        @pl.when(j > 0)
        def _():
            out_ss_ref[...] += part

    if nk == 1:
        y = product()
        if ss_dim:
            y = y * _inv_rms(in_ss_ref, ss_dim)
        if act == "relu2":
            y = jnp.maximum(y, 0.0)
            y = y * y
        if has_res:
            y = res_ref[...] + y
        o_ref[...] = y.astype(o_ref.dtype)
        if n_gain:
            emit_norm_inputs(y)
        return

    k = pl.program_id(2)

    @pl.when(k == 0)
    def _():
        o_ref[...] = res_ref[...] if has_res else jnp.zeros_like(o_ref)

    if not n_gain:
        o_ref[...] += product()
        return

    @pl.when(k < nk - 1)
    def _():
        o_ref[...] += product()

    @pl.when(k == nk - 1)
    def _():
        y = o_ref[...] + product()
        o_ref[...] = y
        emit_norm_inputs(y)


def _linear(lhs_list, w, n_out, out_dtype, *, res=None, act=None, row_ss=None, norm_gains=(),
            tm=1024, tn=1024, tk=4096, single_buffer_lhs=False, side_casts=(), name="linear"):
    m = lhs_list[0].shape[0]
    k_sizes = [a.shape[1] for a in lhs_list]
    k_total = sum(k_sizes)
    w_layer = 0
    if isinstance(w, tuple):
        w_layer, w = w
        w = w.reshape(-1, w.shape[-1])
    assert w.shape[0] % k_total == 0
    tm, tn = math.gcd(tm, m), math.gcd(tn, n_out)
    if len(lhs_list) > 1:
        assert k_total <= tk
        tk = k_total
    else:
        tk = math.gcd(tk, k_total)
        k_sizes = [tk]
    nk = k_total // tk
    assert nk == 1 or (act is None and out_dtype == F32 and row_ss is None)
    n_i, n_j = m // tm, n_out // tn
    n_gain = len(norm_gains)

    def tile(i, j, k):
        return (i, j)

    def row_block(i, j, k):
        return (i, 0)

    lhs_mode = dict(pipeline_mode=pl.Buffered(1)) if single_buffer_lhs else {}
    in_specs = [pl.BlockSpec((tm, ks), lambda i, j, k: (i, k), **lhs_mode) for ks in k_sizes]
    in_specs.append(pl.BlockSpec((tk, tn), lambda i, j, k: (w_layer * nk + k, j)))
    args = list(lhs_list) + [w]
    if res is not None:
        in_specs.append(pl.BlockSpec((tm, tn), tile))
        args.append(res)
    if row_ss is not None:
        in_specs.append(pl.BlockSpec((tm, LANES), row_block))
        args.append(row_ss)
    if n_gain:
        in_specs.append(pl.BlockSpec((n_gain, tn), lambda i, j, k: (0, j)))
        args.append(jnp.stack(norm_gains).astype(F32))
    out_specs = [pl.BlockSpec((tm, tn), tile)] * (1 + n_gain)
    out_shape = [jax.ShapeDtypeStruct((m, n_out), out_dtype)]
    out_shape += [jax.ShapeDtypeStruct((m, n_out), BF16)] * n_gain
    if n_gain:
        out_specs.append(pl.BlockSpec((tm, LANES), row_block))
        out_shape.append(jax.ShapeDtypeStruct((m, LANES), F32))
    side_in, side_args, side_out, side_shapes, finalize = _side_cast_plan(
        side_casts, n_i * n_j * nk, lambda i, j, k: (i * n_j + j) * nk + k)
    kernel = functools.partial(
        _linear_kernel, k_sizes=tuple(k_sizes), nk=nk, act=act, has_res=res is not None,
        ss_dim=k_total if row_ss is not None else 0, n_gain=n_gain, n_side=len(side_casts))
    n_main = len(out_specs)
    outs = pl.pallas_call(
        kernel,
        grid=(n_i, n_j, nk),
        in_specs=in_specs + side_in,
        out_specs=out_specs + side_out,
        out_shape=out_shape + side_shapes,
        compiler_params=_params("parallel", "arbitrary" if n_gain else "parallel", "arbitrary"),
        name=name,
    )(*args, *side_args)
    outs = list(outs[:n_main]) + finalize(outs[n_main:])
    return outs[0] if len(outs) == 1 else tuple(outs)


def _pool_kernel(h_ref, halo_ref, w_ref, wpg_ref, scale_ref, *rest, tm, n_side):
    side_src, (o_ref, *side_dst), (u_ref, p_ref) = (
        rest[:n_side], rest[n_side:2 * n_side + 1], rest[2 * n_side + 1:])
    _run_side_casts(side_src, side_dst)
    i = pl.program_id(1)
    g = pl.program_id(2)
    w = w_ref[...]
    u_ref[POOL_HALO:, :] = jnp.dot(h_ref[...], w, preferred_element_type=F32)
    u_halo = jnp.dot(halo_ref[...], w, preferred_element_type=F32)
    u_ref[:POOL_HALO, :] = jnp.where(i > 0, u_halo, 0.0)
    pos = i * tm + lax.broadcasted_iota(jnp.int32, (tm, 1), 0)

    for group, window in enumerate(POOL_WINDOWS):
        @pl.when(g == group)
        def _(window=window):
            u = u_ref[...]
            s = u
            shift = 1
            while shift < window:
                s = s + pltpu.roll(s, shift, axis=0)
                shift *= 2
            inv_cnt = 1.0 / jnp.minimum(pos + 1, window).astype(F32)
            p_ref[...] = (s[POOL_HALO:] * inv_cnt - u[POOL_HALO:]).astype(p_ref.dtype)

    mixed = jnp.dot(p_ref[...], wpg_ref[...], preferred_element_type=F32) * scale_ref[...]
    o_ref[...] = mixed.astype(o_ref.dtype)


def _pool_branch(h, w_in, w_pg, scale, batch, seq, tm=1024, side_casts=()):
    n, d = h.shape
    groups, gw = w_pg.shape[0], w_pg.shape[1]
    tm = min(tm, seq)
    n_i = seq // tm
    halo_per_tm = tm // POOL_HALO

    def halo_map(b, i, g):
        return (jnp.maximum((b * n_i + i) * halo_per_tm - 1, 0), 0)

    side_in, side_args, side_out, side_shapes, finalize = _side_cast_plan(
        side_casts, batch * n_i * groups, lambda b, i, g: (b * n_i + i) * groups + g)
    outs = pl.pallas_call(
        functools.partial(_pool_kernel, tm=tm, n_side=len(side_casts)),
        grid=(batch, n_i, groups),
        in_specs=[pl.BlockSpec((tm, d), lambda b, i, g: (b * n_i + i, 0)),
                  pl.BlockSpec((POOL_HALO, d), halo_map),
                  pl.BlockSpec((d, gw), lambda b, i, g: (0, g)),
                  pl.BlockSpec((None, gw, gw), lambda b, i, g: (g, 0, 0)),
                  pl.BlockSpec((1, gw), lambda b, i, g: (0, g))] + side_in,
        out_specs=[pl.BlockSpec((tm, gw), lambda b, i, g: (b * n_i + i, g))] + side_out,
        out_shape=[jax.ShapeDtypeStruct((n, groups * gw), BF16)] + side_shapes,
        scratch_shapes=[pltpu.VMEM((POOL_HALO + tm, gw), F32), pltpu.VMEM((tm, gw), BF16)],
        compiler_params=_params("parallel", "parallel", "arbitrary"),
        name="pool_branch",
    )(h, h, w_in, w_pg, scale.reshape(1, -1), *side_args)
    return outs[0] if not side_casts else (outs[0], *finalize(outs[1:]))


def _mem_attn_kernel(h_ref, wq_ref, k_ref, v_ref, *rest, scale, ss_dim, n_side):
    rest = list(rest)
    ss_ref = rest.pop(0) if ss_dim else None
    side_src = [rest.pop(0) for _ in range(n_side)]
    o_ref = rest.pop(0)
    _run_side_casts(side_src, rest)
    wq = wq_ref[...].astype(BF16)
    q = jnp.dot(h_ref[...], wq, preferred_element_type=F32)
    if ss_dim:
        q = q * _inv_rms(ss_ref, ss_dim)
    q = q.astype(BF16)
    s = lax.dot_general(q, k_ref[...], (((1,), (1,)), ((), ())), preferred_element_type=F32) * scale
    m = jnp.max(s, axis=-1, keepdims=True)
    p = jnp.exp(s - m)
    inv_den = 1.0 / jnp.sum(p, axis=-1, keepdims=True)
    o = jnp.dot(p.astype(BF16), v_ref[...], preferred_element_type=F32) * inv_den
    o_ref[...] = o.astype(o_ref.dtype)


def _mem_attn(h, w_in, q_col0, mkv, batch, seq, mem_len, tm=1024, row_ss=None, side_casts=()):
    n, d = h.shape
    hd = mkv.shape[1] // (2 * MEM_HEADS)
    tm = min(tm, seq)
    n_i = seq // tm
    q_blk0 = q_col0 // hd
    assert q_col0 % hd == 0
    in_specs = [pl.BlockSpec((tm, d), lambda b, i, hh: (b * n_i + i, 0)),
                pl.BlockSpec((d, hd), lambda b, i, hh: (0, q_blk0 + hh)),
                pl.BlockSpec((mem_len, hd), lambda b, i, hh: (b, hh)),
                pl.BlockSpec((mem_len, hd), lambda b, i, hh: (b, MEM_HEADS + hh))]
    args = [h, w_in, mkv, mkv]
    if row_ss is not None:
        in_specs.append(pl.BlockSpec((tm, LANES), lambda b, i, hh: (b * n_i + i, 0)))
        args.append(row_ss)
    side_in, side_args, side_out, side_shapes, finalize = _side_cast_plan(
        side_casts, batch * n_i * MEM_HEADS, lambda b, i, hh: (b * n_i + i) * MEM_HEADS + hh)
    outs = pl.pallas_call(
        functools.partial(_mem_attn_kernel, scale=1.0 / math.sqrt(hd),
                          ss_dim=d if row_ss is not None else 0, n_side=len(side_casts)),
        grid=(batch, n_i, MEM_HEADS),
        in_specs=in_specs + side_in,
        out_specs=[pl.BlockSpec((tm, hd), lambda b, i, hh: (b * n_i + i, hh))] + side_out,
        out_shape=[jax.ShapeDtypeStruct((n, MEM_HEADS * hd), BF16)] + side_shapes,
        compiler_params=_params("parallel", "parallel", "arbitrary"),
        name="mem_attn",
    )(*args, *side_args)
    return outs[0] if not side_casts else (outs[0], *finalize(outs[1:]))


def _t5_bucket(dist):
    max_exact = NUM_BUCKETS // 2
    d32 = jnp.maximum(dist, 1).astype(F32)
    large = max_exact + (jnp.log(d32 / max_exact) / math.log(MAX_DISTANCE / max_exact)
                         * (NUM_BUCKETS - max_exact)).astype(jnp.int32)
    large = jnp.minimum(large, NUM_BUCKETS - 1)
    return jnp.where(dist < max_exact, dist, large)


def _bucket_tables():
    qi = jnp.arange(ATT_BLOCK)[:, None]
    kj = jnp.arange(2 * ATT_BLOCK)[None, :]
    delta = qi + ATT_BLOCK - kj
    band = (delta >= 0) & (delta <= ATT_BLOCK)
    tabs = [jnp.where(band, _t5_bucket(jnp.maximum(delta, 0) * dil), -1) for _, dil in DIL_CONFIGS]
    return jnp.stack(tabs).astype(jnp.int32)


def _dil_attn_kernel(rb_ref, bkt_ref, q0, q1, q2, k0, k1, k2, v0, v1, v2, o_ref,
                     og0, og1, og2, lg0, lg1, lg2, *, seq, heads, scale):
    head = pl.program_id(1)
    q_refs, k_refs, v_refs = (q0, q1, q2), (k0, k1, k2), (v0, v1, v2)
    o_nat, lse_nat = (og0, og1, og2), (lg0, lg1, lg2)
    blk = ATT_BLOCK

    for g, (_, dil) in enumerate(DIL_CONFIGS):
        sub_len = seq // dil
        n_blk = sub_len // blk
        bkt = bkt_ref[g]
        bias = jnp.full(bkt.shape, MASKED, F32)
        for b in range(NUM_BUCKETS):
            bias = jnp.where(bkt == b, rb_ref[b, g * heads + head], bias)

        for r in range(dil):
            rows = pl.ds(r, sub_len, stride=dil) if dil > 1 else pl.ds(0, sub_len)
            q_r = q_refs[g][rows, :].astype(BF16)
            k_r = k_refs[g][rows, :].astype(BF16)
            v_r = v_refs[g][rows, :].astype(BF16)
            for n in range(n_blk):
                lo = max(n - 1, 0) * blk
                qb = q_r[n * blk:(n + 1) * blk]
                kk = k_r[lo:(n + 1) * blk]
                vv = v_r[lo:(n + 1) * blk]
                tb = bias if n > 0 else bias[:, blk:]
                s = lax.dot_general(qb, kk, (((1,), (1,)), ((), ())),
                                    preferred_element_type=F32) * scale + tb
                m = jnp.max(s, axis=-1, keepdims=True)
                p = jnp.exp(s - m)
                den = jnp.sum(p, axis=-1, keepdims=True)
                o = jnp.dot(p.astype(BF16), vv, preferred_element_type=F32) * (1.0 / den)
                lse = m + jnp.log(den)
                start = r + dil * n * blk
                nat = pl.ds(start, blk, stride=dil) if dil > 1 else pl.ds(start, blk)
                o_nat[g][nat, :] = o
                lse_nat[g][nat, :] = jnp.broadcast_to(lse, (blk, HEAD_DIM))

    l0, l1, l2 = lse_nat[0][...], lse_nat[1][...], lse_nat[2][...]
    top = jnp.maximum(jnp.maximum(l0, l1), l2)
    w0, w1, w2 = jnp.exp(l0 - top), jnp.exp(l1 - top), jnp.exp(l2 - top)
    inv = 1.0 / (w0 + w1 + w2)
    out = (o_nat[0][...] * w0 + o_nat[1][...] * w1 + o_nat[2][...] * w2) * inv
    o_ref[...] = out.astype(o_ref.dtype)


def _dil_attn(q, kv, rel_bias, batch, seq):
    n = q.shape[0]
    n_groups = len(DIL_CONFIGS)
    heads = rel_bias.shape[1] // n_groups
    gh = n_groups * heads

    def col_spec(off):
        return pl.BlockSpec((seq, HEAD_DIM), lambda b, hh: (b, off + hh))

    in_specs = [pl.BlockSpec(memory_space=pltpu.SMEM),
                pl.BlockSpec((n_groups, ATT_BLOCK, 2 * ATT_BLOCK), lambda b, hh: (0, 0, 0))]
    in_specs += [col_spec(g * heads) for g in range(n_groups)]
    in_specs += [col_spec(g * heads) for g in range(n_groups)]
    in_specs += [col_spec(gh + g * heads) for g in range(n_groups)]
    return pl.pallas_call(
        functools.partial(_dil_attn_kernel, seq=seq, heads=heads, scale=1.0 / math.sqrt(HEAD_DIM)),
        grid=(batch, heads),
        in_specs=in_specs,
        out_specs=pl.BlockSpec((seq, HEAD_DIM), lambda b, hh: (b, hh)),
        out_shape=jax.ShapeDtypeStruct((n, heads * HEAD_DIM), BF16),
        scratch_shapes=[pltpu.VMEM((seq, HEAD_DIM), F32)] * (2 * n_groups),
        compiler_params=_params("parallel", "arbitrary"),
        name="dil_attn",
    )(rel_bias.astype(F32), _bucket_tables(), q, q, q, kv, kv, kv, kv, kv, kv)


def kernel(x, mem, a_norm, a_w_in, a_w_pg, a_scale, a_w_out, kv_norm, w_kv, b_norm, b_w_in,
           b_w_out, mem_norm, w_mem_kv, mlp_norm, mlp_w1, mlp_w2, rel_bias, final_norm):
    batch, seq, d = x.shape
    mem_len = mem.shape[1]
    n = batch * seq
    pool_width = a_scale.shape[-1]
    dil_q_width = w_kv.shape[1] // 2
    assert a_norm.shape[0] == 1 and b_norm.shape[0] == 1 and mlp_norm.shape[0] == 2

    x0 = x.reshape(n, d)
    (mem_h,) = _rmsnorm(mem.reshape(batch * mem_len, d), [mem_norm], BF16)
    mkv = [_linear([mem_h], (l, w_mem_kv), w_mem_kv.shape[2], BF16, tm=2048, tn=512,
                   single_buffer_lhs=True, name="mem_kv") for l in range(2)]

    (h0,) = _rmsnorm(x0, [a_norm[0]], BF16)
    mixed, a_out_w, b_out_w = _pool_branch(
        h0, a_w_in[0, :, :pool_width].astype(BF16), a_w_pg[0].astype(BF16), a_scale[0], batch, seq,
        side_casts=[(0, a_w_out), (0, b_w_out)])
    mem_out, w1 = _mem_attn(h0, a_w_in[0], pool_width, mkv[0], batch, seq, mem_len,
                            side_casts=[(0, mlp_w1)])
    x1, xg, ss = _linear([mixed, mem_out], a_out_w, d, F32, res=x0, norm_gains=[mlp_norm[0]],
                         tn=512, name="a_out")
    hid, w2, w1_next, kv_w, b_in = _linear(
        [xg], w1, mlp_w1.shape[2], BF16, act="relu2", row_ss=ss,
        side_casts=[(0, mlp_w2), (1, mlp_w1), (0, w_kv[None]), (0, b_w_in)], name="mlp_up")
    x2, xg_kv, xg_b, ss = _linear([hid], w2, d, F32, res=x1, norm_gains=[kv_norm, b_norm[0]],
                                  tk=2048, name="mlp_down")

    kv = _linear([xg_kv], kv_w, w_kv.shape[1], F32, row_ss=ss, name="kv_proj")
    q = _linear([xg_b], b_in, dil_q_width, F32, row_ss=ss, name="q_proj")
    dil_out = _dil_attn(q, kv, rel_bias, batch, seq)
    mem_out = _mem_attn(xg_b, b_in, dil_q_width, mkv[1], batch, seq, mem_len, row_ss=ss)
    x3, xg, ss = _linear([dil_out, mem_out], b_out_w, d, F32, res=x2, norm_gains=[mlp_norm[1]],
                         name="b_out")
    hid, w2 = _linear([xg], w1_next, mlp_w1.shape[2], BF16, act="relu2", row_ss=ss,
                      side_casts=[(1, mlp_w2)], name="mlp_up")
    x4 = _linear([hid], w2, d, F32, res=x3, name="mlp_down")

    (out,) = _rmsnorm(x4, [final_norm], F32)
    return out.reshape(batch, seq, d)
```

```python
import functools
import math

import jax
import jax.numpy as jnp
from jax import lax
from jax.experimental import pallas as pl
from jax.experimental.pallas import tpu as pltpu

F32 = jnp.float32
BF16 = jnp.bfloat16

EPS = 1e-6
POOL_WINDOWS = (2, 4, 8, 16)
POOL_HALO = 16
MEM_HEADS = 4
LANES = 128
HEAD_DIM = 128
DIL_CONFIGS = ((128, 1), (512, 4), (2048, 16))
ATT_BLOCK = 128
NUM_BUCKETS = 32
MAX_DISTANCE = 2048
MASKED = -1e30

V7X_VMEM_BYTES = 64 * 1024 * 1024
VMEM_LIMIT_BYTES = V7X_VMEM_BYTES - 2 * 1024 * 1024


def _params(*semantics):
    return pltpu.CompilerParams(dimension_semantics=semantics, vmem_limit_bytes=VMEM_LIMIT_BYTES)


def _rmsnorm_kernel(x_ref, g_ref, *o_refs):
    x = x_ref[...]
    y = x * lax.rsqrt(jnp.mean(x * x, axis=-1, keepdims=True) + EPS)
    for n, o_ref in enumerate(o_refs):
        o_ref[...] = (y * g_ref[n:n + 1, :]).astype(o_ref.dtype)


def _rmsnorm(x, gains, out_dtype, tm=512):
    m, d = x.shape
    n = len(gains)
    tm = math.gcd(tm, m)
    g = jnp.stack(gains).astype(F32)
    outs = pl.pallas_call(
        _rmsnorm_kernel,
        grid=(m // tm,),
        in_specs=[pl.BlockSpec((tm, d), lambda i: (i, 0)),
                  pl.BlockSpec((n, d), lambda i: (0, 0))],
        out_specs=[pl.BlockSpec((tm, d), lambda i: (i, 0))] * n,
        out_shape=[jax.ShapeDtypeStruct((m, d), out_dtype)] * n,
        compiler_params=_params("parallel"),
        name="rmsnorm",
    )(x, g)
    return list(outs)


def _side_cast_plan(side_casts, steps, step_index):
    in_specs, args, out_specs, out_shapes, shapes, flags = [], [], [], [], [], []
    for layer, stack, *gain in side_casts:
        _, rows, cols = stack.shape
        fold = 1
        while (rows * fold) % (steps * 16):
            fold *= 2
        assert cols % (fold * LANES) == 0 and (fold == 1 or not gain)
        r, c = rows * fold, cols // fold
        slab = r // steps
        in_specs.append(pl.BlockSpec(
            (slab, c), lambda *g, layer=layer: (layer * steps + step_index(*g), 0)))
        args.append(stack.reshape(-1, c))
        if gain:
            in_specs.append(pl.BlockSpec((slab, LANES), lambda *g: (step_index(*g), 0)))
            args.append(jnp.broadcast_to(gain[0].astype(F32)[:, None], (rows, LANES)))
        out_specs.append(pl.BlockSpec((slab, c), lambda *g: (step_index(*g), 0)))
        out_shapes.append(jax.ShapeDtypeStruct((r, c), BF16))
        shapes.append((rows, cols))
        flags.append(bool(gain))

    def finalize(outs):
        return [o.reshape(s) for o, s in zip(outs, shapes)]

    return in_specs, args, out_specs, out_shapes, tuple(flags), finalize


def _pop_side_inputs(refs, side_flags):
    jobs = []
    for has_gain in side_flags:
        src_ref = refs.pop(0)
        jobs.append((src_ref, refs.pop(0) if has_gain else None))
    return jobs


def _run_side_casts(jobs, dst_refs):
    for (src_ref, gain_ref), dst_ref in zip(jobs, dst_refs):
        if gain_ref is None:
            dst_ref[...] = src_ref[...].astype(dst_ref.dtype)
            continue
        g = gain_ref[...]
        for c in range(0, src_ref.shape[1], LANES):
            dst_ref[:, c:c + LANES] = (src_ref[:, c:c + LANES] * g).astype(dst_ref.dtype)


def _inv_rms(ss_ref, dim):
    return lax.rsqrt(ss_ref[:, 0:1] * (1.0 / dim) + EPS)


def _linear_kernel(*refs, k_sizes, nk, act, has_res, ss_dim, gain_rows, side_flags):
    refs = list(refs)
    lhs_refs = [refs.pop(0) for _ in k_sizes]
    w_ref = refs.pop(0)
    res_ref = refs.pop(0) if has_res else None
    in_ss_ref = refs.pop(0) if ss_dim else None
    gain_ref = refs.pop(0) if any(r is not None for r in gain_rows) else None
    side_jobs = _pop_side_inputs(refs, side_flags)
    o_ref = refs.pop(0)
    scaled_refs = [refs.pop(0) for _ in gain_rows]
    out_ss_ref = refs.pop(0) if gain_rows else None
    _run_side_casts(side_jobs, refs)
    n_gain = len(gain_rows)
    j = pl.program_id(1)

    def product():
        y, off = None, 0
        for lhs_ref, ks in zip(lhs_refs, k_sizes):
            w = w_ref[off:off + ks, :].astype(BF16)
            t = jnp.dot(lhs_ref[...], w, preferred_element_type=F32)
            y = t if y is None else y + t
            off += ks
        return y

    def emit_norm_inputs(y):
        for row, s_ref in zip(gain_rows, scaled_refs):
            scaled = y if row is None else y * gain_ref[row:row + 1, :]
            s_ref[...] = scaled.astype(s_ref.dtype)
        part = jnp.broadcast_to(jnp.sum(y * y, axis=-1, keepdims=True), out_ss_ref.shape)

        @pl.when(j == 0)
        def _():
            out_ss_ref[...] = part

        @pl.when(j > 0)
        def _():
            out_ss_ref[...] += part

    if nk == 1:
        y = product()
        if ss_dim:
            y = y * _inv_rms(in_ss_ref, ss_dim)
        if act == "relu2":
            y = jnp.maximum(y, 0.0)
            y = y * y
        if has_res:
            y = res_ref[...] + y
        o_ref[...] = y.astype(o_ref.dtype)
        if n_gain:
            emit_norm_inputs(y)
        return

    k = pl.program_id(2)

    @pl.when(k == 0)
    def _():
        o_ref[...] = res_ref[...] if has_res else jnp.zeros_like(o_ref)

    if not n_gain:
        o_ref[...] += product()
        return

    @pl.when(k < nk - 1)
    def _():
        o_ref[...] += product()

    @pl.when(k == nk - 1)
    def _():
        y = o_ref[...] + product()
        o_ref[...] = y
        emit_norm_inputs(y)


def _linear(lhs_list, w, n_out, out_dtype, *, res=None, act=None, row_ss=None, norm_gains=(),
            tm=1024, tn=1024, tk=4096, single_buffer_lhs=False, side_casts=(), name="linear"):
    m = lhs_list[0].shape[0]
    k_sizes = [a.shape[1] for a in lhs_list]
    k_total = sum(k_sizes)
    w_layer = 0
    if isinstance(w, tuple):
        w_layer, w = w
        w = w.reshape(-1, w.shape[-1])
    assert w.shape[0] % k_total == 0
    tm, tn = math.gcd(tm, m), math.gcd(tn, n_out)
    if len(lhs_list) > 1:
        assert k_total <= tk
        tk = k_total
    else:
        tk = math.gcd(tk, k_total)
        k_sizes = [tk]
    nk = k_total // tk
    assert nk == 1 or (act is None and out_dtype == F32 and row_ss is None)
    n_i, n_j = m // tm, n_out // tn
    n_gain = len(norm_gains)

    def tile(i, j, k):
        return (i, j)

    def row_block(i, j, k):
        return (i, 0)

    lhs_mode = dict(pipeline_mode=pl.Buffered(1)) if single_buffer_lhs else {}
    in_specs = [pl.BlockSpec((tm, ks), lambda i, j, k: (i, k), **lhs_mode) for ks in k_sizes]
    in_specs.append(pl.BlockSpec((tk, tn), lambda i, j, k: (w_layer * nk + k, j)))
    args = list(lhs_list) + [w]
    if res is not None:
        in_specs.append(pl.BlockSpec((tm, tn), tile))
        args.append(res)
    if row_ss is not None:
        in_specs.append(pl.BlockSpec((tm, LANES), row_block))
        args.append(row_ss)
    gains = [g for g in norm_gains if g is not None]
    gain_rows = tuple(None if g is None else sum(h is not None for h in norm_gains[:n])
                      for n, g in enumerate(norm_gains))
    if gains:
        in_specs.append(pl.BlockSpec((len(gains), tn), lambda i, j, k: (0, j)))
        args.append(jnp.stack(gains).astype(F32))
    out_specs = [pl.BlockSpec((tm, tn), tile)] * (1 + n_gain)
    out_shape = [jax.ShapeDtypeStruct((m, n_out), out_dtype)]
    out_shape += [jax.ShapeDtypeStruct((m, n_out), BF16)] * n_gain
    if n_gain:
        out_specs.append(pl.BlockSpec((tm, LANES), row_block))
        out_shape.append(jax.ShapeDtypeStruct((m, LANES), F32))
    side_in, side_args, side_out, side_shapes, side_flags, finalize = _side_cast_plan(
        side_casts, n_i * n_j * nk, lambda i, j, k: (i * n_j + j) * nk + k)
    kernel = functools.partial(
        _linear_kernel, k_sizes=tuple(k_sizes), nk=nk, act=act, has_res=res is not None,
        ss_dim=k_total if row_ss is not None else 0, gain_rows=gain_rows, side_flags=side_flags)
    n_main = len(out_specs)
    outs = pl.pallas_call(
        kernel,
        grid=(n_i, n_j, nk),
        in_specs=in_specs + side_in,
        out_specs=out_specs + side_out,
        out_shape=out_shape + side_shapes,
        compiler_params=_params("parallel", "arbitrary" if n_gain else "parallel", "arbitrary"),
        name=name,
    )(*args, *side_args)
    outs = list(outs[:n_main]) + finalize(outs[n_main:])
    return outs[0] if len(outs) == 1 else tuple(outs)


def _pool_kernel(h_ref, halo_ref, w_ref, wpg_ref, scale_ref, *rest, tm, side_flags):
    rest = list(rest)
    side_jobs = _pop_side_inputs(rest, side_flags)
    o_ref = rest.pop(0)
    u_ref, p_ref = rest[len(side_flags):]
    _run_side_casts(side_jobs, rest[:len(side_flags)])
    i = pl.program_id(1)
    g = pl.program_id(2)
    w = w_ref[...]
    u_ref[POOL_HALO:, :] = jnp.dot(h_ref[...], w, preferred_element_type=F32)
    u_halo = jnp.dot(halo_ref[...], w, preferred_element_type=F32)
    u_ref[:POOL_HALO, :] = jnp.where(i > 0, u_halo, 0.0)
    pos = i * tm + lax.broadcasted_iota(jnp.int32, (tm, 1), 0)

    for group, window in enumerate(POOL_WINDOWS):
        @pl.when(g == group)
        def _(window=window):
            u = u_ref[...]
            s = u
            shift = 1
            while shift < window:
                s = s + pltpu.roll(s, shift, axis=0)
                shift *= 2
            inv_cnt = 1.0 / jnp.minimum(pos + 1, window).astype(F32)
            p_ref[...] = (s[POOL_HALO:] * inv_cnt - u[POOL_HALO:]).astype(p_ref.dtype)

    mixed = jnp.dot(p_ref[...], wpg_ref[...], preferred_element_type=F32) * scale_ref[...]
    o_ref[...] = mixed.astype(o_ref.dtype)


def _pool_branch(h, w_in, w_pg, scale, batch, seq, tm=1024, side_casts=()):
    n, d = h.shape
    groups, gw = w_pg.shape[0], w_pg.shape[1]
    tm = min(tm, seq)
    n_i = seq // tm
    halo_per_tm = tm // POOL_HALO

    def halo_map(b, i, g):
        return (jnp.maximum((b * n_i + i) * halo_per_tm - 1, 0), 0)

    side_in, side_args, side_out, side_shapes, side_flags, finalize = _side_cast_plan(
        side_casts, batch * n_i * groups, lambda b, i, g: (b * n_i + i) * groups + g)
    outs = pl.pallas_call(
        functools.partial(_pool_kernel, tm=tm, side_flags=side_flags),
        grid=(batch, n_i, groups),
        in_specs=[pl.BlockSpec((tm, d), lambda b, i, g: (b * n_i + i, 0)),
                  pl.BlockSpec((POOL_HALO, d), halo_map),
                  pl.BlockSpec((d, gw), lambda b, i, g: (0, g)),
                  pl.BlockSpec((None, gw, gw), lambda b, i, g: (g, 0, 0)),
                  pl.BlockSpec((1, gw), lambda b, i, g: (0, g))] + side_in,
        out_specs=[pl.BlockSpec((tm, gw), lambda b, i, g: (b * n_i + i, g))] + side_out,
        out_shape=[jax.ShapeDtypeStruct((n, groups * gw), BF16)] + side_shapes,
        scratch_shapes=[pltpu.VMEM((POOL_HALO + tm, gw), F32), pltpu.VMEM((tm, gw), BF16)],
        compiler_params=_params("parallel", "parallel", "arbitrary"),
        name="pool_branch",
    )(h, h, w_in, w_pg, scale.reshape(1, -1), *side_args)
    return outs[0] if not side_casts else (outs[0], *finalize(outs[1:]))


def _mem_attn_kernel(h_ref, wq_ref, k_ref, v_ref, *rest, scale, ss_dim, side_flags):
    rest = list(rest)
    ss_ref = rest.pop(0) if ss_dim else None
    side_jobs = _pop_side_inputs(rest, side_flags)
    o_ref = rest.pop(0)
    _run_side_casts(side_jobs, rest)
    wq = wq_ref[...].astype(BF16)
    q = jnp.dot(h_ref[...], wq, preferred_element_type=F32)
    if ss_dim:
        q = q * _inv_rms(ss_ref, ss_dim)
    q = q.astype(BF16)
    s = lax.dot_general(q, k_ref[...], (((1,), (1,)), ((), ())), preferred_element_type=F32) * scale
    m = jnp.max(s, axis=-1, keepdims=True)
    p = jnp.exp(s - m)
    inv_den = 1.0 / jnp.sum(p, axis=-1, keepdims=True)
    o = jnp.dot(p.astype(BF16), v_ref[...], preferred_element_type=F32) * inv_den
    o_ref[...] = o.astype(o_ref.dtype)


def _mem_attn(h, w_in, q_col0, mkv, batch, seq, mem_len, tm=1024, row_ss=None, side_casts=()):
    n, d = h.shape
    hd = mkv.shape[1] // (2 * MEM_HEADS)
    tm = min(tm, seq)
    n_i = seq // tm
    q_blk0 = q_col0 // hd
    assert q_col0 % hd == 0
    in_specs = [pl.BlockSpec((tm, d), lambda b, i, hh: (b * n_i + i, 0)),
                pl.BlockSpec((d, hd), lambda b, i, hh: (0, q_blk0 + hh)),
                pl.BlockSpec((mem_len, hd), lambda b, i, hh: (b, hh)),
                pl.BlockSpec((mem_len, hd), lambda b, i, hh: (b, MEM_HEADS + hh))]
    args = [h, w_in, mkv, mkv]
    if row_ss is not None:
        in_specs.append(pl.BlockSpec((tm, LANES), lambda b, i, hh: (b * n_i + i, 0)))
        args.append(row_ss)
    side_in, side_args, side_out, side_shapes, side_flags, finalize = _side_cast_plan(
        side_casts, batch * n_i * MEM_HEADS, lambda b, i, hh: (b * n_i + i) * MEM_HEADS + hh)
    outs = pl.pallas_call(
        functools.partial(_mem_attn_kernel, scale=1.0 / math.sqrt(hd),
                          ss_dim=d if row_ss is not None else 0, side_flags=side_flags),
        grid=(batch, n_i, MEM_HEADS),
        in_specs=in_specs + side_in,
        out_specs=[pl.BlockSpec((tm, hd), lambda b, i, hh: (b * n_i + i, hh))] + side_out,
        out_shape=[jax.ShapeDtypeStruct((n, MEM_HEADS * hd), BF16)] + side_shapes,
        compiler_params=_params("parallel", "parallel", "arbitrary"),
        name="mem_attn",
    )(*args, *side_args)
    return outs[0] if not side_casts else (outs[0], *finalize(outs[1:]))


def _t5_bucket(dist):
    max_exact = NUM_BUCKETS // 2
    d32 = jnp.maximum(dist, 1).astype(F32)
    large = max_exact + (jnp.log(d32 / max_exact) / math.log(MAX_DISTANCE / max_exact)
                         * (NUM_BUCKETS - max_exact)).astype(jnp.int32)
    large = jnp.minimum(large, NUM_BUCKETS - 1)
    return jnp.where(dist < max_exact, dist, large)


def _bucket_tables():
    qi = jnp.arange(ATT_BLOCK)[:, None]
    kj = jnp.arange(2 * ATT_BLOCK)[None, :]
    delta = qi + ATT_BLOCK - kj
    band = (delta >= 0) & (delta <= ATT_BLOCK)
    tabs = [jnp.where(band, _t5_bucket(jnp.maximum(delta, 0) * dil), -1) for _, dil in DIL_CONFIGS]
    return jnp.stack(tabs).astype(jnp.int32)


def _dil_attn_kernel(rb_ref, bkt_ref, q0, q1, q2, k0, k1, k2, v0, v1, v2, o_ref,
                     og0, og1, og2, lg0, lg1, lg2, *, seq, heads, scale):
    head = pl.program_id(1)
    q_refs, k_refs, v_refs = (q0, q1, q2), (k0, k1, k2), (v0, v1, v2)
    o_nat, lse_nat = (og0, og1, og2), (lg0, lg1, lg2)
    blk = ATT_BLOCK

    for g, (_, dil) in enumerate(DIL_CONFIGS):
        sub_len = seq // dil
        n_blk = sub_len // blk
        bkt = bkt_ref[g]
        bias = jnp.full(bkt.shape, MASKED, F32)
        for b in range(NUM_BUCKETS):
            bias = jnp.where(bkt == b, rb_ref[b, g * heads + head], bias)

        for r in range(dil):
            rows = pl.ds(r, sub_len, stride=dil) if dil > 1 else pl.ds(0, sub_len)
            q_r = q_refs[g][rows, :].astype(BF16)
            k_r = k_refs[g][rows, :].astype(BF16)
            v_r = v_refs[g][rows, :].astype(BF16)
            for n in range(n_blk):
                lo = max(n - 1, 0) * blk
                qb = q_r[n * blk:(n + 1) * blk]
                kk = k_r[lo:(n + 1) * blk]
                vv = v_r[lo:(n + 1) * blk]
                tb = bias if n > 0 else bias[:, blk:]
                s = lax.dot_general(qb, kk, (((1,), (1,)), ((), ())),
                                    preferred_element_type=F32) * scale + tb
                m = jnp.max(s, axis=-1, keepdims=True)
                p = jnp.exp(s - m)
                den = jnp.sum(p, axis=-1, keepdims=True)
                o = jnp.dot(p.astype(BF16), vv, preferred_element_type=F32) * (1.0 / den)
                lse = m + jnp.log(den)
                start = r + dil * n * blk
                nat = pl.ds(start, blk, stride=dil) if dil > 1 else pl.ds(start, blk)
                o_nat[g][nat, :] = o
                lse_nat[g][nat, :] = jnp.broadcast_to(lse, (blk, HEAD_DIM))

    l0, l1, l2 = lse_nat[0][...], lse_nat[1][...], lse_nat[2][...]
    top = jnp.maximum(jnp.maximum(l0, l1), l2)
    w0, w1, w2 = jnp.exp(l0 - top), jnp.exp(l1 - top), jnp.exp(l2 - top)
    inv = 1.0 / (w0 + w1 + w2)
    out = (o_nat[0][...] * w0 + o_nat[1][...] * w1 + o_nat[2][...] * w2) * inv
    o_ref[...] = out.astype(o_ref.dtype)


def _dil_attn(q, kv, rel_bias, batch, seq):
    n = q.shape[0]
    n_groups = len(DIL_CONFIGS)
    heads = rel_bias.shape[1] // n_groups
    gh = n_groups * heads

    def col_spec(off):
        return pl.BlockSpec((seq, HEAD_DIM), lambda b, hh: (b, off + hh))

    in_specs = [pl.BlockSpec(memory_space=pltpu.SMEM),
                pl.BlockSpec((n_groups, ATT_BLOCK, 2 * ATT_BLOCK), lambda b, hh: (0, 0, 0))]
    in_specs += [col_spec(g * heads) for g in range(n_groups)]
    in_specs += [col_spec(g * heads) for g in range(n_groups)]
    in_specs += [col_spec(gh + g * heads) for g in range(n_groups)]
    return pl.pallas_call(
        functools.partial(_dil_attn_kernel, seq=seq, heads=heads, scale=1.0 / math.sqrt(HEAD_DIM)),
        grid=(batch, heads),
        in_specs=in_specs,
        out_specs=pl.BlockSpec((seq, HEAD_DIM), lambda b, hh: (b, hh)),
        out_shape=jax.ShapeDtypeStruct((n, heads * HEAD_DIM), BF16),
        scratch_shapes=[pltpu.VMEM((seq, HEAD_DIM), F32)] * (2 * n_groups),
        compiler_params=_params("parallel", "arbitrary"),
        name="dil_attn",
    )(rel_bias.astype(F32), _bucket_tables(), q, q, q, kv, kv, kv, kv, kv, kv)


def kernel(x, mem, a_norm, a_w_in, a_w_pg, a_scale, a_w_out, kv_norm, w_kv, b_norm, b_w_in,
           b_w_out, mem_norm, w_mem_kv, mlp_norm, mlp_w1, mlp_w2, rel_bias, final_norm):
    batch, seq, d = x.shape
    mem_len = mem.shape[1]
    n = batch * seq
    pool_width = a_scale.shape[-1]
    dil_q_width = w_kv.shape[1] // 2
    assert a_norm.shape[0] == 1 and b_norm.shape[0] == 1 and mlp_norm.shape[0] == 2

    x0 = x.reshape(n, d)
    (mem_h,) = _rmsnorm(mem.reshape(batch * mem_len, d), [mem_norm], BF16)
    mkv = [_linear([mem_h], (l, w_mem_kv), w_mem_kv.shape[2], BF16, tm=2048, tn=512,
                   single_buffer_lhs=True, name="mem_kv") for l in range(2)]

    (h0,) = _rmsnorm(x0, [a_norm[0]], BF16)
    mixed, a_out_w, b_out_w = _pool_branch(
        h0, a_w_in[0, :, :pool_width].astype(BF16), a_w_pg[0].astype(BF16), a_scale[0], batch, seq,
        side_casts=[(0, a_w_out), (0, b_w_out)])
    mem_out, w1 = _mem_attn(h0, a_w_in[0], pool_width, mkv[0], batch, seq, mem_len,
                            side_casts=[(0, mlp_w1)])
    x1, xg, ss = _linear([mixed, mem_out], a_out_w, d, F32, res=x0, norm_gains=[mlp_norm[0]],
                         tn=512, name="a_out")
    hid, w2, w1_next, kv_w, b_in = _linear(
        [xg], w1, mlp_w1.shape[2], BF16, act="relu2", row_ss=ss,
        side_casts=[(0, mlp_w2), (1, mlp_w1), (0, w_kv[None], kv_norm), (0, b_w_in, b_norm[0])],
        name="mlp_up")
    x2, xn, ss = _linear([hid], w2, d, F32, res=x1, norm_gains=[None], name="mlp_down")

    kv = _linear([xn], kv_w, w_kv.shape[1], F32, row_ss=ss, name="kv_proj")
    q = _linear([xn], b_in, dil_q_width, F32, row_ss=ss, name="q_proj")
    dil_out = _dil_attn(q, kv, rel_bias, batch, seq)
    mem_out = _mem_attn(xn, b_in, dil_q_width, mkv[1], batch, seq, mem_len, row_ss=ss)
    x3, xg, ss = _linear([dil_out, mem_out], b_out_w, d, F32, res=x2, norm_gains=[mlp_norm[1]],
                         name="b_out")
    hid, w2 = _linear([xg], w1_next, mlp_w1.shape[2], BF16, act="relu2", row_ss=ss,
                      side_casts=[(1, mlp_w2)], name="mlp_up")
    x4 = _linear([hid], w2, d, F32, res=x3, name="mlp_down")

    (out,) = _rmsnorm(x4, [final_norm], F32)
    return out.reshape(batch, seq, d)
```

```python
import functools
import math

import jax
import jax.numpy as jnp
from jax import lax
from jax.experimental import pallas as pl
from jax.experimental.pallas import tpu as pltpu

F32 = jnp.float32
BF16 = jnp.bfloat16

EPS = 1e-6
POOL_WINDOWS = (2, 4, 8, 16)
POOL_HALO = 16
POOL_BLOCK = 128
MEM_HEADS = 4
LANES = 128
HEAD_DIM = 128
DIL_CONFIGS = ((128, 1), (512, 4), (2048, 16))
ATT_BLOCK = 128
NUM_BUCKETS = 32
MAX_DISTANCE = 2048
MASKED = -1e30

V7X_VMEM_BYTES = 64 * 1024 * 1024
VMEM_LIMIT_BYTES = V7X_VMEM_BYTES - 2 * 1024 * 1024


def _params(*semantics):
    return pltpu.CompilerParams(dimension_semantics=semantics, vmem_limit_bytes=VMEM_LIMIT_BYTES)


def _rmsnorm_kernel(x_ref, g_ref, *o_refs):
    x = x_ref[...]
    y = x * lax.rsqrt(jnp.mean(x * x, axis=-1, keepdims=True) + EPS)
    for n, o_ref in enumerate(o_refs):
        o_ref[...] = (y * g_ref[n:n + 1, :]).astype(o_ref.dtype)


def _rmsnorm(x, gains, out_dtype, tm=512):
    m, d = x.shape
    n = len(gains)
    tm = math.gcd(tm, m)
    g = jnp.stack(gains).astype(F32)
    outs = pl.pallas_call(
        _rmsnorm_kernel,
        grid=(m // tm,),
        in_specs=[pl.BlockSpec((tm, d), lambda i: (i, 0)),
                  pl.BlockSpec((n, d), lambda i: (0, 0))],
        out_specs=[pl.BlockSpec((tm, d), lambda i: (i, 0))] * n,
        out_shape=[jax.ShapeDtypeStruct((m, d), out_dtype)] * n,
        compiler_params=_params("parallel"),
        name="rmsnorm",
    )(x, g)
    return list(outs)


def _side_cast_plan(side_casts, steps, step_index):
    in_specs, args, out_specs, out_shapes, shapes, flags = [], [], [], [], [], []
    for layer, stack, *gain in side_casts:
        _, rows, cols = stack.shape
        fold = 1
        while (rows * fold) % (steps * 16):
            fold *= 2
        assert cols % (fold * LANES) == 0 and (fold == 1 or not gain)
        r, c = rows * fold, cols // fold
        slab = r // steps
        in_specs.append(pl.BlockSpec(
            (slab, c), lambda *g, layer=layer: (layer * steps + step_index(*g), 0)))
        args.append(stack.reshape(-1, c))
        if gain:
            in_specs.append(pl.BlockSpec((slab, LANES), lambda *g: (step_index(*g), 0)))
            args.append(jnp.broadcast_to(gain[0].astype(F32)[:, None], (rows, LANES)))
        out_specs.append(pl.BlockSpec((slab, c), lambda *g: (step_index(*g), 0)))
        out_shapes.append(jax.ShapeDtypeStruct((r, c), BF16))
        shapes.append((rows, cols))
        flags.append(bool(gain))

    def finalize(outs):
        return [o.reshape(s) for o, s in zip(outs, shapes)]

    return in_specs, args, out_specs, out_shapes, tuple(flags), finalize


def _pop_side_inputs(refs, side_flags):
    jobs = []
    for has_gain in side_flags:
        src_ref = refs.pop(0)
        jobs.append((src_ref, refs.pop(0) if has_gain else None))
    return jobs


def _run_side_casts(jobs, dst_refs):
    for (src_ref, gain_ref), dst_ref in zip(jobs, dst_refs):
        if gain_ref is None:
            dst_ref[...] = src_ref[...].astype(dst_ref.dtype)
            continue
        g = gain_ref[...]
        for c in range(0, src_ref.shape[1], LANES):
            dst_ref[:, c:c + LANES] = (src_ref[:, c:c + LANES] * g).astype(dst_ref.dtype)


def _inv_rms(ss_ref, dim):
    return lax.rsqrt(ss_ref[:, 0:1] * (1.0 / dim) + EPS)


def _linear_kernel(*refs, k_sizes, nk, act, has_res, ss_dim, gain_rows, side_flags):
    refs = list(refs)
    lhs_refs = [refs.pop(0) for _ in k_sizes]
    w_ref = refs.pop(0)
    res_ref = refs.pop(0) if has_res else None
    in_ss_ref = refs.pop(0) if ss_dim else None
    gain_ref = refs.pop(0) if any(r is not None for r in gain_rows) else None
    side_jobs = _pop_side_inputs(refs, side_flags)
    o_ref = refs.pop(0)
    scaled_refs = [refs.pop(0) for _ in gain_rows]
    out_ss_ref = refs.pop(0) if gain_rows else None
    _run_side_casts(side_jobs, refs)
    n_gain = len(gain_rows)
    j = pl.program_id(1)

    def product():
        y, off = None, 0
        for lhs_ref, ks in zip(lhs_refs, k_sizes):
            w = w_ref[off:off + ks, :].astype(BF16)
            t = jnp.dot(lhs_ref[...], w, preferred_element_type=F32)
            y = t if y is None else y + t
            off += ks
        return y

    def emit_norm_inputs(y):
        for row, s_ref in zip(gain_rows, scaled_refs):
            scaled = y if row is None else y * gain_ref[row:row + 1, :]
            s_ref[...] = scaled.astype(s_ref.dtype)
        part = jnp.broadcast_to(jnp.sum(y * y, axis=-1, keepdims=True), out_ss_ref.shape)

        @pl.when(j == 0)
        def _():
            out_ss_ref[...] = part

        @pl.when(j > 0)
        def _():
            out_ss_ref[...] += part

    if nk == 1:
        y = product()
        if ss_dim:
            y = y * _inv_rms(in_ss_ref, ss_dim)
        if act == "relu2":
            y = jnp.maximum(y, 0.0)
            y = y * y
        if has_res:
            y = res_ref[...] + y
        o_ref[...] = y.astype(o_ref.dtype)
        if n_gain:
            emit_norm_inputs(y)
        return

    k = pl.program_id(2)

    @pl.when(k == 0)
    def _():
        o_ref[...] = res_ref[...] if has_res else jnp.zeros_like(o_ref)

    if not n_gain:
        o_ref[...] += product()
        return

    @pl.when(k < nk - 1)
    def _():
        o_ref[...] += product()

    @pl.when(k == nk - 1)
    def _():
        y = o_ref[...] + product()
        o_ref[...] = y
        emit_norm_inputs(y)


def _linear(lhs_list, w, n_out, out_dtype, *, res=None, act=None, row_ss=None, norm_gains=(),
            tm=1024, tn=1024, tk=4096, single_buffer_lhs=False, side_casts=(), name="linear"):
    m = lhs_list[0].shape[0]
    k_sizes = [a.shape[1] for a in lhs_list]
    k_total = sum(k_sizes)
    w_layer = 0
    if isinstance(w, tuple):
        w_layer, w = w
        w = w.reshape(-1, w.shape[-1])
    assert w.shape[0] % k_total == 0
    tm, tn = math.gcd(tm, m), math.gcd(tn, n_out)
    if len(lhs_list) > 1:
        assert k_total <= tk
        tk = k_total
    else:
        tk = math.gcd(tk, k_total)
        k_sizes = [tk]
    nk = k_total // tk
    assert nk == 1 or (act is None and out_dtype == F32 and row_ss is None)
    n_i, n_j = m // tm, n_out // tn
    n_gain = len(norm_gains)

    def tile(i, j, k):
        return (i, j)

    def row_block(i, j, k):
        return (i, 0)

    lhs_mode = dict(pipeline_mode=pl.Buffered(1)) if single_buffer_lhs else {}
    in_specs = [pl.BlockSpec((tm, ks), lambda i, j, k: (i, k), **lhs_mode) for ks in k_sizes]
    in_specs.append(pl.BlockSpec((tk, tn), lambda i, j, k: (w_layer * nk + k, j)))
    args = list(lhs_list) + [w]
    if res is not None:
        in_specs.append(pl.BlockSpec((tm, tn), tile))
        args.append(res)
    if row_ss is not None:
        in_specs.append(pl.BlockSpec((tm, LANES), row_block))
        args.append(row_ss)
    gains = [g for g in norm_gains if g is not None]
    gain_rows = tuple(None if g is None else sum(h is not None for h in norm_gains[:n])
                      for n, g in enumerate(norm_gains))
    if gains:
        in_specs.append(pl.BlockSpec((len(gains), tn), lambda i, j, k: (0, j)))
        args.append(jnp.stack(gains).astype(F32))
    out_specs = [pl.BlockSpec((tm, tn), tile)] * (1 + n_gain)
    out_shape = [jax.ShapeDtypeStruct((m, n_out), out_dtype)]
    out_shape += [jax.ShapeDtypeStruct((m, n_out), BF16)] * n_gain
    if n_gain:
        out_specs.append(pl.BlockSpec((tm, LANES), row_block))
        out_shape.append(jax.ShapeDtypeStruct((m, LANES), F32))
    side_in, side_args, side_out, side_shapes, side_flags, finalize = _side_cast_plan(
        side_casts, n_i * n_j * nk, lambda i, j, k: (i * n_j + j) * nk + k)
    kernel = functools.partial(
        _linear_kernel, k_sizes=tuple(k_sizes), nk=nk, act=act, has_res=res is not None,
        ss_dim=k_total if row_ss is not None else 0, gain_rows=gain_rows, side_flags=side_flags)
    n_main = len(out_specs)
    outs = pl.pallas_call(
        kernel,
        grid=(n_i, n_j, nk),
        in_specs=in_specs + side_in,
        out_specs=out_specs + side_out,
        out_shape=out_shape + side_shapes,
        compiler_params=_params("parallel", "arbitrary" if n_gain else "parallel", "arbitrary"),
        name=name,
    )(*args, *side_args)
    outs = list(outs[:n_main]) + finalize(outs[n_main:])
    return outs[0] if len(outs) == 1 else tuple(outs)


def _pool_kernel(h_ref, w_ref, wpg_ref, scale_ref, *rest, tm, side_flags):
    rest = list(rest)
    side_jobs = _pop_side_inputs(rest, side_flags)
    o_ref = rest.pop(0)
    u_ref, ub_ref, p_ref, carry_ref = rest[len(side_flags):]
    _run_side_casts(side_jobs, rest[:len(side_flags)])
    i = pl.program_id(1)
    g = pl.program_id(2)
    blk = POOL_BLOCK
    half = tm // 2
    u_ref[:half, :] = jnp.dot(h_ref[:half, :], w_ref[...], preferred_element_type=F32)
    u_ref[half:, :] = jnp.dot(h_ref[half:, :], w_ref[...], preferred_element_type=F32)
    ub_ref[:blk - POOL_HALO, :] = jnp.zeros((blk - POOL_HALO, ub_ref.shape[1]), ub_ref.dtype)

    @pl.when(i == 0)
    def _():
        ub_ref[blk - POOL_HALO:blk, :] = jnp.zeros((POOL_HALO, ub_ref.shape[1]), ub_ref.dtype)

    @pl.when(i > 0)
    def _():
        ub_ref[blk - POOL_HALO:blk, :] = carry_ref[g]

    ub_ref[blk:, :] = u_ref[...].astype(ub_ref.dtype)
    carry_ref[g] = ub_ref[blk + tm - POOL_HALO:, :]

    window = jnp.left_shift(2, g)
    row = lax.broadcasted_iota(jnp.int32, (blk, 2 * blk), 0) + blk
    col = lax.broadcasted_iota(jnp.int32, (blk, 2 * blk), 1)
    band = jnp.where((col <= row) & (col > row - window), 1.0, 0.0).astype(BF16)
    pos = i * tm + lax.broadcasted_iota(jnp.int32, (tm, 1), 0)
    inv_cnt = 1.0 / jnp.minimum(pos + 1, window).astype(F32)
    for r in range(tm // blk):
        rows = slice(r * blk, (r + 1) * blk)
        s = jnp.dot(band, ub_ref[r * blk:(r + 2) * blk, :], preferred_element_type=F32)
        p_ref[rows, :] = (s * inv_cnt[rows] - u_ref[rows, :]).astype(p_ref.dtype)

    mixed = jnp.dot(p_ref[...], wpg_ref[...], preferred_element_type=F32) * scale_ref[...]
    o_ref[...] = mixed.astype(o_ref.dtype)


def _pool_branch(h, w_in, w_pg, scale, batch, seq, tm=1024, side_casts=()):
    n, d = h.shape
    groups, gw = w_pg.shape[0], w_pg.shape[1]
    assert POOL_WINDOWS == tuple(2 << g for g in range(groups)) and max(POOL_WINDOWS) <= POOL_HALO
    tm = min(tm, seq)
    n_i = seq // tm
    side_in, side_args, side_out, side_shapes, side_flags, finalize = _side_cast_plan(
        side_casts, batch * n_i * groups, lambda b, i, g: (b * n_i + i) * groups + g)
    outs = pl.pallas_call(
        functools.partial(_pool_kernel, tm=tm, side_flags=side_flags),
        grid=(batch, n_i, groups),
        in_specs=[pl.BlockSpec((tm, d), lambda b, i, g: (b * n_i + i, 0)),
                  pl.BlockSpec((d, gw), lambda b, i, g: (0, g)),
                  pl.BlockSpec((None, gw, gw), lambda b, i, g: (g, 0, 0)),
                  pl.BlockSpec((1, gw), lambda b, i, g: (0, g))] + side_in,
        out_specs=[pl.BlockSpec((tm, gw), lambda b, i, g: (b * n_i + i, g))] + side_out,
        out_shape=[jax.ShapeDtypeStruct((n, groups * gw), BF16)] + side_shapes,
        scratch_shapes=[pltpu.VMEM((tm, gw), F32), pltpu.VMEM((POOL_BLOCK + tm, gw), BF16),
                        pltpu.VMEM((tm, gw), BF16), pltpu.VMEM((groups, POOL_HALO, gw), BF16)],
        compiler_params=_params("parallel", "arbitrary", "arbitrary"),
        name="pool_branch",
    )(h, w_in, w_pg, scale.reshape(1, -1), *side_args)
    return outs[0] if not side_casts else (outs[0], *finalize(outs[1:]))


def _mem_attn_kernel(h_ref, wq_ref, k_ref, v_ref, *rest, scale, ss_dim, side_flags):
    rest = list(rest)
    ss_ref = rest.pop(0) if ss_dim else None
    side_jobs = _pop_side_inputs(rest, side_flags)
    o_ref = rest.pop(0)
    _run_side_casts(side_jobs, rest)
    wq = wq_ref[...].astype(BF16)
    q = jnp.dot(h_ref[...], wq, preferred_element_type=F32)
    if ss_dim:
        q = q * _inv_rms(ss_ref, ss_dim)
    q = q.astype(BF16)
    s = lax.dot_general(q, k_ref[...], (((1,), (1,)), ((), ())), preferred_element_type=F32) * scale
    m = jnp.max(s, axis=-1, keepdims=True)
    p = jnp.exp(s - m)
    inv_den = 1.0 / jnp.sum(p, axis=-1, keepdims=True)
    o = jnp.dot(p.astype(BF16), v_ref[...], preferred_element_type=F32) * inv_den
    o_ref[...] = o.astype(o_ref.dtype)


def _mem_attn(h, w_in, q_col0, mkv, batch, seq, mem_len, tm=1024, row_ss=None, side_casts=()):
    n, d = h.shape
    hd = mkv.shape[1] // (2 * MEM_HEADS)
    tm = min(tm, seq)
    n_i = seq // tm
    q_blk0 = q_col0 // hd
    assert q_col0 % hd == 0
    in_specs = [pl.BlockSpec((tm, d), lambda b, i, hh: (b * n_i + i, 0)),
                pl.BlockSpec((d, hd), lambda b, i, hh: (0, q_blk0 + hh)),
                pl.BlockSpec((mem_len, hd), lambda b, i, hh: (b, hh)),
                pl.BlockSpec((mem_len, hd), lambda b, i, hh: (b, MEM_HEADS + hh))]
    args = [h, w_in, mkv, mkv]
    if row_ss is not None:
        in_specs.append(pl.BlockSpec((tm, LANES), lambda b, i, hh: (b * n_i + i, 0)))
        args.append(row_ss)
    side_in, side_args, side_out, side_shapes, side_flags, finalize = _side_cast_plan(
        side_casts, batch * n_i * MEM_HEADS, lambda b, i, hh: (b * n_i + i) * MEM_HEADS + hh)
    outs = pl.pallas_call(
        functools.partial(_mem_attn_kernel, scale=1.0 / math.sqrt(hd),
                          ss_dim=d if row_ss is not None else 0, side_flags=side_flags),
        grid=(batch, n_i, MEM_HEADS),
        in_specs=in_specs + side_in,
        out_specs=[pl.BlockSpec((tm, hd), lambda b, i, hh: (b * n_i + i, hh))] + side_out,
        out_shape=[jax.ShapeDtypeStruct((n, MEM_HEADS * hd), BF16)] + side_shapes,
        compiler_params=_params("parallel", "parallel", "arbitrary"),
        name="mem_attn",
    )(*args, *side_args)
    return outs[0] if not side_casts else (outs[0], *finalize(outs[1:]))


def _t5_bucket(dist):
    max_exact = NUM_BUCKETS // 2
    d32 = jnp.maximum(dist, 1).astype(F32)
    large = max_exact + (jnp.log(d32 / max_exact) / math.log(MAX_DISTANCE / max_exact)
                         * (NUM_BUCKETS - max_exact)).astype(jnp.int32)
    large = jnp.minimum(large, NUM_BUCKETS - 1)
    return jnp.where(dist < max_exact, dist, large)


def _bucket_tables():
    qi = jnp.arange(ATT_BLOCK)[:, None]
    kj = jnp.arange(2 * ATT_BLOCK)[None, :]
    delta = qi + ATT_BLOCK - kj
    band = (delta >= 0) & (delta <= ATT_BLOCK)
    tabs = [jnp.where(band, _t5_bucket(jnp.maximum(delta, 0) * dil), -1) for _, dil in DIL_CONFIGS]
    return jnp.stack(tabs).astype(jnp.int32)


def _bias_table_kernel(rb_ref, bkt_ref, o_ref):
    gh = pl.program_id(0)
    bkt = bkt_ref[...]
    bias = jnp.full(bkt.shape, MASKED, F32)
    for b in range(NUM_BUCKETS):
        bias = jnp.where(bkt == b, rb_ref[b, gh], bias)
    o_ref[...] = bias


def _bias_tables(rel_bias, heads):
    n_gh = rel_bias.shape[1]
    tab = (ATT_BLOCK, 2 * ATT_BLOCK)
    return pl.pallas_call(
        _bias_table_kernel,
        grid=(n_gh,),
        in_specs=[pl.BlockSpec(memory_space=pltpu.SMEM),
                  pl.BlockSpec((None,) + tab, lambda gh: (gh // heads, 0, 0))],
        out_specs=pl.BlockSpec((None,) + tab, lambda gh: (gh, 0, 0)),
        out_shape=jax.ShapeDtypeStruct((n_gh,) + tab, F32),
        compiler_params=_params("parallel"),
        name="bias_tables",
    )(rel_bias.astype(F32), _bucket_tables())


def _dil_attn_kernel(b0, b1, b2, q0, q1, q2, k0, k1, k2, v0, v1, v2, o_ref,
                     og0, og1, og2, lg0, lg1, lg2, *, seq, scale):
    bias_refs = (b0, b1, b2)
    q_refs, k_refs, v_refs = (q0, q1, q2), (k0, k1, k2), (v0, v1, v2)
    o_nat, lse_nat = (og0, og1, og2), (lg0, lg1, lg2)
    blk = ATT_BLOCK

    for g, (_, dil) in enumerate(DIL_CONFIGS):
        sub_len = seq // dil
        n_blk = sub_len // blk
        bias = bias_refs[g][...]

        for r in range(dil):
            rows = pl.ds(r, sub_len, stride=dil) if dil > 1 else pl.ds(0, sub_len)
            q_r = q_refs[g][rows, :].astype(BF16)
            k_r = k_refs[g][rows, :].astype(BF16)
            v_r = v_refs[g][rows, :].astype(BF16)
            for n in range(n_blk):
                lo = max(n - 1, 0) * blk
                qb = q_r[n * blk:(n + 1) * blk]
                kk = k_r[lo:(n + 1) * blk]
                vv = v_r[lo:(n + 1) * blk]
                tb = bias if n > 0 else bias[:, blk:]
                s = lax.dot_general(qb, kk, (((1,), (1,)), ((), ())),
                                    preferred_element_type=F32) * scale + tb
                m = jnp.max(s, axis=-1, keepdims=True)
                p = jnp.exp(s - m)
                den = jnp.sum(p, axis=-1, keepdims=True)
                o = jnp.dot(p.astype(BF16), vv, preferred_element_type=F32) * (1.0 / den)
                lse = m + jnp.log(den)
                start = r + dil * n * blk
                nat = pl.ds(start, blk, stride=dil) if dil > 1 else pl.ds(start, blk)
                o_nat[g][nat, :] = o
                lse_nat[g][nat, :] = jnp.broadcast_to(lse, (blk, HEAD_DIM))

    l0, l1, l2 = lse_nat[0][...], lse_nat[1][...], lse_nat[2][...]
    top = jnp.maximum(jnp.maximum(l0, l1), l2)
    w0, w1, w2 = jnp.exp(l0 - top), jnp.exp(l1 - top), jnp.exp(l2 - top)
    inv = 1.0 / (w0 + w1 + w2)
    out = (o_nat[0][...] * w0 + o_nat[1][...] * w1 + o_nat[2][...] * w2) * inv
    o_ref[...] = out.astype(o_ref.dtype)


def _dil_attn(q, kv, rel_bias, batch, seq):
    n = q.shape[0]
    n_groups = len(DIL_CONFIGS)
    heads = rel_bias.shape[1] // n_groups
    gh = n_groups * heads
    bias = _bias_tables(rel_bias, heads)

    def col_spec(off):
        return pl.BlockSpec((seq, HEAD_DIM), lambda b, hh: (b, off + hh))

    in_specs = [pl.BlockSpec((None, ATT_BLOCK, 2 * ATT_BLOCK),
                             lambda b, hh, g=g: (g * heads + hh, 0, 0)) for g in range(n_groups)]
    in_specs += [col_spec(g * heads) for g in range(n_groups)]
    in_specs += [col_spec(g * heads) for g in range(n_groups)]
    in_specs += [col_spec(gh + g * heads) for g in range(n_groups)]
    return pl.pallas_call(
        functools.partial(_dil_attn_kernel, seq=seq, scale=1.0 / math.sqrt(HEAD_DIM)),
        grid=(batch, heads),
        in_specs=in_specs,
        out_specs=pl.BlockSpec((seq, HEAD_DIM), lambda b, hh: (b, hh)),
        out_shape=jax.ShapeDtypeStruct((n, heads * HEAD_DIM), BF16),
        scratch_shapes=[pltpu.VMEM((seq, HEAD_DIM), F32)] * (2 * n_groups),
        compiler_params=_params("parallel", "arbitrary"),
        name="dil_attn",
    )(bias, bias, bias, q, q, q, kv, kv, kv, kv, kv, kv)


def kernel(x, mem, a_norm, a_w_in, a_w_pg, a_scale, a_w_out, kv_norm, w_kv, b_norm, b_w_in,
           b_w_out, mem_norm, w_mem_kv, mlp_norm, mlp_w1, mlp_w2, rel_bias, final_norm):
    batch, seq, d = x.shape
    mem_len = mem.shape[1]
    n = batch * seq
    pool_width = a_scale.shape[-1]
    dil_q_width = w_kv.shape[1] // 2
    assert a_norm.shape[0] == 1 and b_norm.shape[0] == 1 and mlp_norm.shape[0] == 2

    x0 = x.reshape(n, d)
    (mem_h,) = _rmsnorm(mem.reshape(batch * mem_len, d), [mem_norm], BF16)
    mkv = [_linear([mem_h], (l, w_mem_kv), w_mem_kv.shape[2], BF16, tm=2048, tn=512,
                   single_buffer_lhs=True, name="mem_kv") for l in range(2)]

    (h0,) = _rmsnorm(x0, [a_norm[0]], BF16)
    mixed, a_out_w, b_out_w = _pool_branch(
        h0, a_w_in[0, :, :pool_width].astype(BF16), a_w_pg[0].astype(BF16), a_scale[0], batch, seq,
        side_casts=[(0, a_w_out), (0, b_w_out)])
    mem_out, w1 = _mem_attn(h0, a_w_in[0], pool_width, mkv[0], batch, seq, mem_len,
                            side_casts=[(0, mlp_w1)])
    x1, xg, ss = _linear([mixed, mem_out], a_out_w, d, F32, res=x0, norm_gains=[mlp_norm[0]],
                         tn=512, name="a_out")
    hid, w2, w1_next, kv_w, b_in = _linear(
        [xg], w1, mlp_w1.shape[2], BF16, act="relu2", row_ss=ss,
        side_casts=[(0, mlp_w2), (1, mlp_w1), (0, w_kv[None], kv_norm), (0, b_w_in, b_norm[0])],
        name="mlp_up")
    x2, xn, ss = _linear([hid], w2, d, F32, res=x1, norm_gains=[None], name="mlp_down")

    kv = _linear([xn], kv_w, w_kv.shape[1], F32, row_ss=ss, name="kv_proj")
    q = _linear([xn], b_in, dil_q_width, F32, row_ss=ss, name="q_proj")
    dil_out = _dil_attn(q, kv, rel_bias, batch, seq)
    mem_out = _mem_attn(xn, b_in, dil_q_width, mkv[1], batch, seq, mem_len, row_ss=ss)
    x3, xg, ss = _linear([dil_out, mem_out], b_out_w, d, F32, res=x2, norm_gains=[mlp_norm[1]],
                         name="b_out")
    hid, w2 = _linear([xg], w1_next, mlp_w1.shape[2], BF16, act="relu2", row_ss=ss,
                      side_casts=[(1, mlp_w2)], name="mlp_up")
    x4 = _linear([hid], w2, d, F32, res=x3, name="mlp_down")

    (out,) = _rmsnorm(x4, [final_norm], F32)
    return out.reshape(batch, seq, d)
```

```python
import functools
import math

import jax
import jax.numpy as jnp
from jax import lax
from jax.experimental import pallas as pl
from jax.experimental.pallas import tpu as pltpu

F32 = jnp.float32
BF16 = jnp.bfloat16

EPS = 1e-6
POOL_WINDOWS = (2, 4, 8, 16)
POOL_HALO = 16
POOL_BLOCK = 128
MEM_HEADS = 4
LANES = 128
HEAD_DIM = 128
DIL_CONFIGS = ((128, 1), (512, 4), (2048, 16))
ATT_BLOCK = 128
NUM_BUCKETS = 32
MAX_DISTANCE = 2048
MASKED = -1e30

V7X_VMEM_BYTES = 64 * 1024 * 1024
VMEM_LIMIT_BYTES = V7X_VMEM_BYTES - 2 * 1024 * 1024


def _params(*semantics):
    return pltpu.CompilerParams(dimension_semantics=semantics, vmem_limit_bytes=VMEM_LIMIT_BYTES)


def _rmsnorm_kernel(x_ref, g_ref, *o_refs):
    x = x_ref[...]
    y = x * lax.rsqrt(jnp.mean(x * x, axis=-1, keepdims=True) + EPS)
    for n, o_ref in enumerate(o_refs):
        o_ref[...] = (y * g_ref[n:n + 1, :]).astype(o_ref.dtype)


def _rmsnorm(x, gains, out_dtype, tm=512):
    m, d = x.shape
    n = len(gains)
    tm = math.gcd(tm, m)
    g = jnp.stack(gains).astype(F32)
    outs = pl.pallas_call(
        _rmsnorm_kernel,
        grid=(m // tm,),
        in_specs=[pl.BlockSpec((tm, d), lambda i: (i, 0)),
                  pl.BlockSpec((n, d), lambda i: (0, 0))],
        out_specs=[pl.BlockSpec((tm, d), lambda i: (i, 0))] * n,
        out_shape=[jax.ShapeDtypeStruct((m, d), out_dtype)] * n,
        compiler_params=_params("parallel"),
        name="rmsnorm",
    )(x, g)
    return list(outs)


def _side_cast_plan(side_casts, steps, step_index):
    in_specs, args, out_specs, out_shapes, shapes, flags = [], [], [], [], [], []
    for layer, stack, *gain in side_casts:
        _, rows, cols = stack.shape
        fold = 1
        while (rows * fold) % (steps * 16):
            fold *= 2
        assert cols % (fold * LANES) == 0 and (fold == 1 or not gain)
        r, c = rows * fold, cols // fold
        slab = r // steps
        in_specs.append(pl.BlockSpec(
            (slab, c), lambda *g, layer=layer: (layer * steps + step_index(*g), 0)))
        args.append(stack.reshape(-1, c))
        if gain:
            in_specs.append(pl.BlockSpec((slab, LANES), lambda *g: (step_index(*g), 0)))
            args.append(jnp.broadcast_to(gain[0].astype(F32)[:, None], (rows, LANES)))
        out_specs.append(pl.BlockSpec((slab, c), lambda *g: (step_index(*g), 0)))
        out_shapes.append(jax.ShapeDtypeStruct((r, c), BF16))
        shapes.append((rows, cols))
        flags.append(bool(gain))

    def finalize(outs):
        return [o.reshape(s) for o, s in zip(outs, shapes)]

    return in_specs, args, out_specs, out_shapes, tuple(flags), finalize


def _pop_side_inputs(refs, side_flags):
    jobs = []
    for has_gain in side_flags:
        src_ref = refs.pop(0)
        jobs.append((src_ref, refs.pop(0) if has_gain else None))
    return jobs


def _run_side_casts(jobs, dst_refs):
    for (src_ref, gain_ref), dst_ref in zip(jobs, dst_refs):
        if gain_ref is None:
            dst_ref[...] = src_ref[...].astype(dst_ref.dtype)
            continue
        g = gain_ref[...]
        for c in range(0, src_ref.shape[1], LANES):
            dst_ref[:, c:c + LANES] = (src_ref[:, c:c + LANES] * g).astype(dst_ref.dtype)


def _inv_rms(ss_ref, dim):
    return lax.rsqrt(ss_ref[:, 0:1] * (1.0 / dim) + EPS)


def _linear_kernel(*refs, k_sizes, nk, act, has_res, ss_dim, gain_rows, side_flags):
    refs = list(refs)
    lhs_refs = [refs.pop(0) for _ in k_sizes]
    w_ref = refs.pop(0)
    res_ref = refs.pop(0) if has_res else None
    in_ss_ref = refs.pop(0) if ss_dim else None
    gain_ref = refs.pop(0) if any(r is not None for r in gain_rows) else None
    side_jobs = _pop_side_inputs(refs, side_flags)
    o_ref = refs.pop(0)
    scaled_refs = [refs.pop(0) for _ in gain_rows]
    out_ss_ref = refs.pop(0) if gain_rows else None
    n_gain = len(gain_rows)
    _run_side_casts(side_jobs, refs)
    j = pl.program_id(1)

    def product():
        y, off = None, 0
        for lhs_ref, ks in zip(lhs_refs, k_sizes):
            w = w_ref[off:off + ks, :].astype(BF16)
            t = jnp.dot(lhs_ref[...], w, preferred_element_type=F32)
            y = t if y is None else y + t
            off += ks
        return y

    def emit_norm_inputs(y):
        for row, s_ref in zip(gain_rows, scaled_refs):
            scaled = y if row is None else y * gain_ref[row:row + 1, :]
            s_ref[...] = scaled.astype(s_ref.dtype)
        part = jnp.broadcast_to(jnp.sum(y * y, axis=-1, keepdims=True), out_ss_ref.shape)

        @pl.when(j == 0)
        def _():
            out_ss_ref[...] = part

        @pl.when(j > 0)
        def _():
            out_ss_ref[...] += part

    if nk == 1:
        y = product()
        if ss_dim:
            y = y * _inv_rms(in_ss_ref, ss_dim)
        if act == "relu2":
            y = jnp.maximum(y, 0.0)
            y = y * y
        if has_res:
            y = res_ref[...] + y
        o_ref[...] = y.astype(o_ref.dtype)
        if n_gain:
            emit_norm_inputs(y)
        return

    k = pl.program_id(2)

    @pl.when(k == 0)
    def _():
        o_ref[...] = res_ref[...] if has_res else jnp.zeros_like(o_ref)

    if not n_gain:
        o_ref[...] += product()
        return

    @pl.when(k < nk - 1)
    def _():
        o_ref[...] += product()

    @pl.when(k == nk - 1)
    def _():
        y = o_ref[...] + product()
        o_ref[...] = y
        emit_norm_inputs(y)


def _linear(lhs_list, w, n_out, out_dtype, *, res=None, act=None, row_ss=None, norm_gains=(),
            tm=1024, tn=1024, tk=4096, single_buffer_lhs=False, side_casts=(), name="linear"):
    m = lhs_list[0].shape[0]
    k_sizes = [a.shape[1] for a in lhs_list]
    k_total = sum(k_sizes)
    w_layer = 0
    if isinstance(w, tuple):
        w_layer, w = w
        w = w.reshape(-1, w.shape[-1])
    assert w.shape[0] % k_total == 0
    tm, tn = math.gcd(tm, m), math.gcd(tn, n_out)
    if len(lhs_list) > 1:
        assert k_total <= tk
        tk = k_total
    else:
        tk = math.gcd(tk, k_total)
        k_sizes = [tk]
    nk = k_total // tk
    assert nk == 1 or (act is None and out_dtype == F32 and row_ss is None)
    n_i, n_j = m // tm, n_out // tn
    n_gain = len(norm_gains)

    def tile(i, j, k):
        return (i, j)

    def row_block(i, j, k):
        return (i, 0)

    lhs_mode = dict(pipeline_mode=pl.Buffered(1)) if single_buffer_lhs else {}
    in_specs = [pl.BlockSpec((tm, ks), lambda i, j, k: (i, k), **lhs_mode) for ks in k_sizes]
    in_specs.append(pl.BlockSpec((tk, tn), lambda i, j, k: (w_layer * nk + k, j)))
    args = list(lhs_list) + [w]
    if res is not None:
        in_specs.append(pl.BlockSpec((tm, tn), tile))
        args.append(res)
    if row_ss is not None:
        in_specs.append(pl.BlockSpec((tm, LANES), row_block))
        args.append(row_ss)
    gains = [g for g in norm_gains if g is not None]
    gain_rows = tuple(None if g is None else sum(h is not None for h in norm_gains[:n])
                      for n, g in enumerate(norm_gains))
    if gains:
        in_specs.append(pl.BlockSpec((len(gains), tn), lambda i, j, k: (0, j)))
        args.append(jnp.stack(gains).astype(F32))
    out_specs = [pl.BlockSpec((tm, tn), tile)] * (1 + n_gain)
    out_shape = [jax.ShapeDtypeStruct((m, n_out), out_dtype)]
    out_shape += [jax.ShapeDtypeStruct((m, n_out), BF16)] * n_gain
    if n_gain:
        out_specs.append(pl.BlockSpec((tm, LANES), row_block))
        out_shape.append(jax.ShapeDtypeStruct((m, LANES), F32))
    side_in, side_args, side_out, side_shapes, side_flags, finalize = _side_cast_plan(
        side_casts, n_i * n_j * nk, lambda i, j, k: (i * n_j + j) * nk + k)
    kernel = functools.partial(
        _linear_kernel, k_sizes=tuple(k_sizes), nk=nk, act=act, has_res=res is not None,
        ss_dim=k_total if row_ss is not None else 0, gain_rows=gain_rows, side_flags=side_flags)
    n_main = len(out_specs)
    outs = pl.pallas_call(
        kernel,
        grid=(n_i, n_j, nk),
        in_specs=in_specs + side_in,
        out_specs=out_specs + side_out,
        out_shape=out_shape + side_shapes,
        compiler_params=_params("parallel", "arbitrary" if n_gain else "parallel", "arbitrary"),
        name=name,
    )(*args, *side_args)
    outs = list(outs[:n_main]) + finalize(outs[n_main:])
    return outs[0] if len(outs) == 1 else tuple(outs)


def _pool_kernel(h_ref, w_ref, wpg_ref, scale_ref, *rest, tm, side_flags):
    rest = list(rest)
    side_jobs = _pop_side_inputs(rest, side_flags)
    o_ref = rest.pop(0)
    u_ref, ub_ref, p_ref, carry_ref = rest[len(side_flags):]
    _run_side_casts(side_jobs, rest[:len(side_flags)])
    i = pl.program_id(1)
    g = pl.program_id(2)
    blk = POOL_BLOCK
    half = tm // 2
    u_ref[:half, :] = jnp.dot(h_ref[:half, :], w_ref[...], preferred_element_type=F32)
    u_ref[half:, :] = jnp.dot(h_ref[half:, :], w_ref[...], preferred_element_type=F32)
    ub_ref[:blk - POOL_HALO, :] = jnp.zeros((blk - POOL_HALO, ub_ref.shape[1]), ub_ref.dtype)

    @pl.when(i == 0)
    def _():
        ub_ref[blk - POOL_HALO:blk, :] = jnp.zeros((POOL_HALO, ub_ref.shape[1]), ub_ref.dtype)

    @pl.when(i > 0)
    def _():
        ub_ref[blk - POOL_HALO:blk, :] = carry_ref[g]

    ub_ref[blk:, :] = u_ref[...].astype(ub_ref.dtype)
    carry_ref[g] = ub_ref[blk + tm - POOL_HALO:, :]

    window = jnp.left_shift(2, g)
    row = lax.broadcasted_iota(jnp.int32, (blk, 2 * blk), 0) + blk
    col = lax.broadcasted_iota(jnp.int32, (blk, 2 * blk), 1)
    band = jnp.where((col <= row) & (col > row - window), 1.0, 0.0).astype(BF16)
    pos = i * tm + lax.broadcasted_iota(jnp.int32, (tm, 1), 0)
    inv_cnt = 1.0 / jnp.minimum(pos + 1, window).astype(F32)
    for r in range(tm // blk):
        rows = slice(r * blk, (r + 1) * blk)
        s = jnp.dot(band, ub_ref[r * blk:(r + 2) * blk, :], preferred_element_type=F32)
        p_ref[rows, :] = (s * inv_cnt[rows] - u_ref[rows, :]).astype(p_ref.dtype)

    mixed = jnp.dot(p_ref[...], wpg_ref[...], preferred_element_type=F32) * scale_ref[...]
    o_ref[...] = mixed.astype(o_ref.dtype)


def _pool_branch(h, w_in, w_pg, scale, batch, seq, tm=1024, side_casts=()):
    n, d = h.shape
    groups, gw = w_pg.shape[0], w_pg.shape[1]
    assert POOL_WINDOWS == tuple(2 << g for g in range(groups)) and max(POOL_WINDOWS) <= POOL_HALO
    tm = min(tm, seq)
    n_i = seq // tm
    side_in, side_args, side_out, side_shapes, side_flags, finalize = _side_cast_plan(
        side_casts, batch * n_i * groups, lambda b, i, g: (b * n_i + i) * groups + g)
    outs = pl.pallas_call(
        functools.partial(_pool_kernel, tm=tm, side_flags=side_flags),
        grid=(batch, n_i, groups),
        in_specs=[pl.BlockSpec((tm, d), lambda b, i, g: (b * n_i + i, 0)),
                  pl.BlockSpec((d, gw), lambda b, i, g: (0, g)),
                  pl.BlockSpec((None, gw, gw), lambda b, i, g: (g, 0, 0)),
                  pl.BlockSpec((1, gw), lambda b, i, g: (0, g))] + side_in,
        out_specs=[pl.BlockSpec((tm, gw), lambda b, i, g: (b * n_i + i, g))] + side_out,
        out_shape=[jax.ShapeDtypeStruct((n, groups * gw), BF16)] + side_shapes,
        scratch_shapes=[pltpu.VMEM((tm, gw), F32), pltpu.VMEM((POOL_BLOCK + tm, gw), BF16),
                        pltpu.VMEM((tm, gw), BF16), pltpu.VMEM((groups, POOL_HALO, gw), BF16)],
        compiler_params=_params("parallel", "arbitrary", "arbitrary"),
        name="pool_branch",
    )(h, w_in, w_pg, scale.reshape(1, -1), *side_args)
    return outs[0] if not side_casts else (outs[0], *finalize(outs[1:]))


def _mem_attn_kernel(h_ref, wq_ref, k_ref, v_ref, *rest, scale, ss_dim, side_flags):
    rest = list(rest)
    ss_ref = rest.pop(0) if ss_dim else None
    side_jobs = _pop_side_inputs(rest, side_flags)
    o_ref = rest.pop(0)
    _run_side_casts(side_jobs, rest)
    wq = wq_ref[...].astype(BF16)
    q = jnp.dot(h_ref[...], wq, preferred_element_type=F32)
    if ss_dim:
        q = q * _inv_rms(ss_ref, ss_dim)
    q = q.astype(BF16)
    s = lax.dot_general(q, k_ref[...], (((1,), (1,)), ((), ())), preferred_element_type=F32) * scale
    m = jnp.max(s, axis=-1, keepdims=True)
    p = jnp.exp(s - m)
    inv_den = 1.0 / jnp.sum(p, axis=-1, keepdims=True)
    o = jnp.dot(p.astype(BF16), v_ref[...], preferred_element_type=F32) * inv_den
    o_ref[...] = o.astype(o_ref.dtype)


def _mem_attn(h, w_in, q_col0, mkv, batch, seq, mem_len, tm=1024, row_ss=None, side_casts=()):
    n, d = h.shape
    hd = mkv.shape[1] // (2 * MEM_HEADS)
    tm = min(tm, seq)
    n_i = seq // tm
    q_blk0 = q_col0 // hd
    assert q_col0 % hd == 0
    in_specs = [pl.BlockSpec((tm, d), lambda b, i, hh: (b * n_i + i, 0)),
                pl.BlockSpec((d, hd), lambda b, i, hh: (0, q_blk0 + hh)),
                pl.BlockSpec((mem_len, hd), lambda b, i, hh: (b, hh)),
                pl.BlockSpec((mem_len, hd), lambda b, i, hh: (b, MEM_HEADS + hh))]
    args = [h, w_in, mkv, mkv]
    if row_ss is not None:
        in_specs.append(pl.BlockSpec((tm, LANES), lambda b, i, hh: (b * n_i + i, 0)))
        args.append(row_ss)
    side_in, side_args, side_out, side_shapes, side_flags, finalize = _side_cast_plan(
        side_casts, batch * n_i * MEM_HEADS, lambda b, i, hh: (b * n_i + i) * MEM_HEADS + hh)
    outs = pl.pallas_call(
        functools.partial(_mem_attn_kernel, scale=1.0 / math.sqrt(hd),
                          ss_dim=d if row_ss is not None else 0, side_flags=side_flags),
        grid=(batch, n_i, MEM_HEADS),
        in_specs=in_specs + side_in,
        out_specs=[pl.BlockSpec((tm, hd), lambda b, i, hh: (b * n_i + i, hh))] + side_out,
        out_shape=[jax.ShapeDtypeStruct((n, MEM_HEADS * hd), BF16)] + side_shapes,
        compiler_params=_params("parallel", "parallel", "arbitrary"),
        name="mem_attn",
    )(*args, *side_args)
    return outs[0] if not side_casts else (outs[0], *finalize(outs[1:]))


def _t5_bucket(dist):
    max_exact = NUM_BUCKETS // 2
    d32 = jnp.maximum(dist, 1).astype(F32)
    large = max_exact + (jnp.log(d32 / max_exact) / math.log(MAX_DISTANCE / max_exact)
                         * (NUM_BUCKETS - max_exact)).astype(jnp.int32)
    large = jnp.minimum(large, NUM_BUCKETS - 1)
    return jnp.where(dist < max_exact, dist, large)


def _bucket_tables():
    qi = jnp.arange(ATT_BLOCK)[:, None]
    kj = jnp.arange(2 * ATT_BLOCK)[None, :]
    delta = qi + ATT_BLOCK - kj
    band = (delta >= 0) & (delta <= ATT_BLOCK)
    tabs = [jnp.where(band, _t5_bucket(jnp.maximum(delta, 0) * dil), -1) for _, dil in DIL_CONFIGS]
    return jnp.stack(tabs).astype(jnp.int32)


def _bias_table_kernel(rb_ref, bkt_ref, o_ref):
    gh = pl.program_id(0)
    bkt = bkt_ref[...]
    bias = jnp.full(bkt.shape, MASKED, F32)
    for b in range(NUM_BUCKETS):
        bias = jnp.where(bkt == b, rb_ref[b, gh], bias)
    o_ref[...] = bias


def _bias_tables(rel_bias, heads):
    n_gh = rel_bias.shape[1]
    tab = (ATT_BLOCK, 2 * ATT_BLOCK)
    return pl.pallas_call(
        _bias_table_kernel,
        grid=(n_gh,),
        in_specs=[pl.BlockSpec(memory_space=pltpu.SMEM),
                  pl.BlockSpec((None,) + tab, lambda gh: (gh // heads, 0, 0))],
        out_specs=pl.BlockSpec((None,) + tab, lambda gh: (gh, 0, 0)),
        out_shape=jax.ShapeDtypeStruct((n_gh,) + tab, F32),
        compiler_params=_params("parallel"),
        name="bias_tables",
    )(rel_bias.astype(F32), _bucket_tables())


def _dil_attn_kernel(b0, b1, b2, q0, q1, q2, k0, k1, k2, v0, v1, v2, o_ref,
                     og0, og1, og2, lg0, lg1, lg2, *, seq, scale):
    bias_refs = (b0, b1, b2)
    q_refs, k_refs, v_refs = (q0, q1, q2), (k0, k1, k2), (v0, v1, v2)
    o_nat, lse_nat = (og0, og1, og2), (lg0, lg1, lg2)
    blk = ATT_BLOCK

    def qk(q3, k3):
        return jnp.einsum("uqd,ukd->uqk", q3, k3, preferred_element_type=F32) * scale

    def pv(p3, v3):
        return jnp.einsum("uqk,ukd->uqd", p3.astype(BF16), v3, preferred_element_type=F32)

    for g, (_, dil) in enumerate(DIL_CONFIGS):
        sub_len = seq // dil
        n_blk = sub_len // blk
        units = dil * n_blk

        def blocks(ref):
            parts = []
            for r in range(dil):
                rows = pl.ds(r, sub_len, stride=dil) if dil > 1 else pl.ds(0, sub_len)
                parts.append(ref[rows, :].astype(BF16).reshape(n_blk, blk, HEAD_DIM))
            return parts[0] if dil == 1 else jnp.concatenate(parts, axis=0)

        q3, k3, v3 = blocks(q_refs[g]), blocks(k_refs[g]), blocks(v_refs[g])
        bias = bias_refs[g][...]
        s_cur = qk(q3, k3) + bias[None, :, blk:]
        m = jnp.max(s_cur, axis=-1, keepdims=True)
        if n_blk > 1:
            k_prev = jnp.concatenate([k3[:1], k3[:-1]], axis=0)
            v_prev = jnp.concatenate([v3[:1], v3[:-1]], axis=0)
            unit = lax.broadcasted_iota(jnp.int32, (units, 1, 1), 0)
            bias_prev = jnp.where(unit % n_blk == 0, MASKED, bias[None, :, :blk])
            s_prev = qk(q3, k_prev) + bias_prev
            m = jnp.maximum(m, jnp.max(s_prev, axis=-1, keepdims=True))
        p_cur = jnp.exp(s_cur - m)
        den = jnp.sum(p_cur, axis=-1, keepdims=True)
        acc = pv(p_cur, v3)
        if n_blk > 1:
            p_prev = jnp.exp(s_prev - m)
            den = den + jnp.sum(p_prev, axis=-1, keepdims=True)
            acc = acc + pv(p_prev, v_prev)
        o3 = acc * (1.0 / den)
        lse3 = m + jnp.log(den)
        for u in range(units):
            r, n = divmod(u, n_blk)
            start = r + dil * n * blk
            nat = pl.ds(start, blk, stride=dil) if dil > 1 else pl.ds(start, blk)
            o_nat[g][nat, :] = o3[u]
            lse_nat[g][nat, :] = jnp.broadcast_to(lse3[u], (blk, HEAD_DIM))

    l0, l1, l2 = lse_nat[0][...], lse_nat[1][...], lse_nat[2][...]
    top = jnp.maximum(jnp.maximum(l0, l1), l2)
    w0, w1, w2 = jnp.exp(l0 - top), jnp.exp(l1 - top), jnp.exp(l2 - top)
    inv = 1.0 / (w0 + w1 + w2)
    out = (o_nat[0][...] * w0 + o_nat[1][...] * w1 + o_nat[2][...] * w2) * inv
    o_ref[...] = out.astype(o_ref.dtype)


def _dil_attn(q, kv, rel_bias, batch, seq):
    n = q.shape[0]
    n_groups = len(DIL_CONFIGS)
    heads = rel_bias.shape[1] // n_groups
    gh = n_groups * heads
    bias = _bias_tables(rel_bias, heads)

    def col_spec(off):
        return pl.BlockSpec((seq, HEAD_DIM), lambda b, hh: (b, off + hh))

    in_specs = [pl.BlockSpec((None, ATT_BLOCK, 2 * ATT_BLOCK),
                             lambda b, hh, g=g: (g * heads + hh, 0, 0)) for g in range(n_groups)]
    in_specs += [col_spec(g * heads) for g in range(n_groups)]
    in_specs += [col_spec(g * heads) for g in range(n_groups)]
    in_specs += [col_spec(gh + g * heads) for g in range(n_groups)]
    return pl.pallas_call(
        functools.partial(_dil_attn_kernel, seq=seq, scale=1.0 / math.sqrt(HEAD_DIM)),
        grid=(batch, heads),
        in_specs=in_specs,
        out_specs=pl.BlockSpec((seq, HEAD_DIM), lambda b, hh: (b, hh)),
        out_shape=jax.ShapeDtypeStruct((n, heads * HEAD_DIM), BF16),
        scratch_shapes=[pltpu.VMEM((seq, HEAD_DIM), F32)] * (2 * n_groups),
        compiler_params=_params("parallel", "arbitrary"),
        name="dil_attn",
    )(bias, bias, bias, q, q, q, kv, kv, kv, kv, kv, kv)


def kernel(x, mem, a_norm, a_w_in, a_w_pg, a_scale, a_w_out, kv_norm, w_kv, b_norm, b_w_in,
           b_w_out, mem_norm, w_mem_kv, mlp_norm, mlp_w1, mlp_w2, rel_bias, final_norm):
    batch, seq, d = x.shape
    mem_len = mem.shape[1]
    n = batch * seq
    pool_width = a_scale.shape[-1]
    dil_q_width = w_kv.shape[1] // 2
    assert a_norm.shape[0] == 1 and b_norm.shape[0] == 1 and mlp_norm.shape[0] == 2

    x0 = x.reshape(n, d)
    (mem_h,) = _rmsnorm(mem.reshape(batch * mem_len, d), [mem_norm], BF16)
    mkv = [_linear([mem_h], (l, w_mem_kv), w_mem_kv.shape[2], BF16, tm=2048, tn=512,
                   single_buffer_lhs=True, name="mem_kv") for l in range(2)]

    (h0,) = _rmsnorm(x0, [a_norm[0]], BF16)
    mixed, a_out_w, b_out_w = _pool_branch(
        h0, a_w_in[0, :, :pool_width].astype(BF16), a_w_pg[0].astype(BF16), a_scale[0], batch, seq,
        side_casts=[(0, a_w_out), (0, b_w_out)])
    mem_out, w1 = _mem_attn(h0, a_w_in[0], pool_width, mkv[0], batch, seq, mem_len,
                            side_casts=[(0, mlp_w1)])
    x1, xg, ss = _linear([mixed, mem_out], a_out_w, d, F32, res=x0, norm_gains=[mlp_norm[0]],
                         tn=512, name="a_out")
    hid, w2, w1_next, kv_w, b_in = _linear(
        [xg], w1, mlp_w1.shape[2], BF16, act="relu2", row_ss=ss,
        side_casts=[(0, mlp_w2), (1, mlp_w1), (0, w_kv[None], kv_norm), (0, b_w_in, b_norm[0])],
        name="mlp_up")
    x2, xn, ss = _linear([hid], w2, d, F32, res=x1, norm_gains=[None], name="mlp_down")

    kv = _linear([xn], kv_w, w_kv.shape[1], F32, row_ss=ss, name="kv_proj")
    q = _linear([xn], b_in, dil_q_width, F32, row_ss=ss, name="q_proj")
    dil_out = _dil_attn(q, kv, rel_bias, batch, seq)
    mem_out = _mem_attn(xn, b_in, dil_q_width, mkv[1], batch, seq, mem_len, row_ss=ss)
    x3, xg, ss = _linear([dil_out, mem_out], b_out_w, d, F32, res=x2, norm_gains=[mlp_norm[1]],
                         name="b_out")
    hid, w2 = _linear([xg], w1_next, mlp_w1.shape[2], BF16, act="relu2", row_ss=ss,
                      side_casts=[(1, mlp_w2)], name="mlp_up")
    x4 = _linear([hid], w2, d, F32, res=x3, name="mlp_down")

    (out,) = _rmsnorm(x4, [final_norm], F32)
    return out.reshape(batch, seq, d)
```

```python
import functools
import math

import jax
import jax.numpy as jnp
from jax import lax
from jax.experimental import pallas as pl
from jax.experimental.pallas import tpu as pltpu

F32 = jnp.float32
BF16 = jnp.bfloat16

EPS = 1e-6
POOL_WINDOWS = (2, 4, 8, 16)
POOL_HALO = 16
POOL_BLOCK = 128
MEM_HEADS = 4
LANES = 128
HEAD_DIM = 128
DIL_CONFIGS = ((128, 1), (512, 4), (2048, 16))
ATT_BLOCK = 128
NUM_BUCKETS = 32
MAX_DISTANCE = 2048
MASKED = -1e30

V7X_VMEM_BYTES = 64 * 1024 * 1024
VMEM_LIMIT_BYTES = V7X_VMEM_BYTES - 2 * 1024 * 1024


def _params(*semantics):
    return pltpu.CompilerParams(dimension_semantics=semantics, vmem_limit_bytes=VMEM_LIMIT_BYTES)


def _rmsnorm_kernel(x_ref, g_ref, *o_refs):
    x = x_ref[...]
    y = x * lax.rsqrt(jnp.mean(x * x, axis=-1, keepdims=True) + EPS)
    for n, o_ref in enumerate(o_refs):
        o_ref[...] = (y * g_ref[n:n + 1, :]).astype(o_ref.dtype)


def _rmsnorm(x, gains, out_dtype, tm=512):
    m, d = x.shape
    n = len(gains)
    tm = math.gcd(tm, m)
    g = jnp.stack(gains).astype(F32)
    outs = pl.pallas_call(
        _rmsnorm_kernel,
        grid=(m // tm,),
        in_specs=[pl.BlockSpec((tm, d), lambda i: (i, 0)),
                  pl.BlockSpec((n, d), lambda i: (0, 0))],
        out_specs=[pl.BlockSpec((tm, d), lambda i: (i, 0))] * n,
        out_shape=[jax.ShapeDtypeStruct((m, d), out_dtype)] * n,
        compiler_params=_params("parallel"),
        name="rmsnorm",
    )(x, g)
    return list(outs)


def _side_cast_plan(side_casts, steps, step_index):
    in_specs, args, out_specs, out_shapes, shapes, flags = [], [], [], [], [], []
    for layer, stack, *gain in side_casts:
        _, rows, cols = stack.shape
        fold = 1
        while (rows * fold) % (steps * 16):
            fold *= 2
        assert cols % (fold * LANES) == 0 and (fold == 1 or not gain)
        r, c = rows * fold, cols // fold
        slab = r // steps
        in_specs.append(pl.BlockSpec(
            (slab, c), lambda *g, layer=layer: (layer * steps + step_index(*g), 0)))
        args.append(stack.reshape(-1, c))
        if gain:
            in_specs.append(pl.BlockSpec((slab, LANES), lambda *g: (step_index(*g), 0)))
            args.append(jnp.broadcast_to(gain[0].astype(F32)[:, None], (rows, LANES)))
        out_specs.append(pl.BlockSpec((slab, c), lambda *g: (step_index(*g), 0)))
        out_shapes.append(jax.ShapeDtypeStruct((r, c), BF16))
        shapes.append((rows, cols))
        flags.append(bool(gain))

    def finalize(outs):
        return [o.reshape(s) for o, s in zip(outs, shapes)]

    return in_specs, args, out_specs, out_shapes, tuple(flags), finalize


def _pop_side_inputs(refs, side_flags):
    jobs = []
    for has_gain in side_flags:
        src_ref = refs.pop(0)
        jobs.append((src_ref, refs.pop(0) if has_gain else None))
    return jobs


def _run_side_casts(jobs, dst_refs):
    for (src_ref, gain_ref), dst_ref in zip(jobs, dst_refs):
        if gain_ref is None:
            dst_ref[...] = src_ref[...].astype(dst_ref.dtype)
            continue
        g = gain_ref[...]
        for c in range(0, src_ref.shape[1], LANES):
            dst_ref[:, c:c + LANES] = (src_ref[:, c:c + LANES] * g).astype(dst_ref.dtype)


def _inv_rms(ss_ref, dim):
    return lax.rsqrt(ss_ref[:, 0:1] * (1.0 / dim) + EPS)


def _linear_kernel(*refs, k_sizes, nk, act, has_res, ss_dim, gain_rows, side_flags):
    refs = list(refs)
    lhs_refs = [refs.pop(0) for _ in k_sizes]
    w_ref = refs.pop(0)
    res_ref = refs.pop(0) if has_res else None
    in_ss_ref = refs.pop(0) if ss_dim else None
    gain_ref = refs.pop(0) if any(r is not None for r in gain_rows) else None
    side_jobs = _pop_side_inputs(refs, side_flags)
    o_ref = refs.pop(0)
    scaled_refs = [refs.pop(0) for _ in gain_rows]
    out_ss_ref = refs.pop(0) if gain_rows else None
    n_gain = len(gain_rows)
    _run_side_casts(side_jobs, refs)
    j = pl.program_id(1)

    def product():
        y, off = None, 0
        for lhs_ref, ks in zip(lhs_refs, k_sizes):
            w = w_ref[off:off + ks, :].astype(BF16)
            t = jnp.dot(lhs_ref[...], w, preferred_element_type=F32)
            y = t if y is None else y + t
            off += ks
        return y

    def emit_norm_inputs(y):
        for row, s_ref in zip(gain_rows, scaled_refs):
            scaled = y if row is None else y * gain_ref[row:row + 1, :]
            s_ref[...] = scaled.astype(s_ref.dtype)
        part = jnp.broadcast_to(jnp.sum(y * y, axis=-1, keepdims=True), out_ss_ref.shape)

        @pl.when(j == 0)
        def _():
            out_ss_ref[...] = part

        @pl.when(j > 0)
        def _():
            out_ss_ref[...] += part

    if nk == 1:
        y = product()
        if ss_dim:
            y = y * _inv_rms(in_ss_ref, ss_dim)
        if act == "relu2":
            y = jnp.maximum(y, 0.0)
            y = y * y
        if has_res:
            y = res_ref[...] + y
        o_ref[...] = y.astype(o_ref.dtype)
        if n_gain:
            emit_norm_inputs(y)
        return

    k = pl.program_id(2)

    @pl.when(k == 0)
    def _():
        o_ref[...] = res_ref[...] if has_res else jnp.zeros_like(o_ref)

    if not n_gain:
        o_ref[...] += product()
        return

    @pl.when(k < nk - 1)
    def _():
        o_ref[...] += product()

    @pl.when(k == nk - 1)
    def _():
        y = o_ref[...] + product()
        o_ref[...] = y
        emit_norm_inputs(y)


def _linear(lhs_list, w, n_out, out_dtype, *, res=None, act=None, row_ss=None, norm_gains=(),
            tm=1024, tn=1024, tk=4096, single_buffer_lhs=False, side_casts=(), name="linear"):
    m = lhs_list[0].shape[0]
    k_sizes = [a.shape[1] for a in lhs_list]
    k_total = sum(k_sizes)
    w_layer = 0
    if isinstance(w, tuple):
        w_layer, w = w
        w = w.reshape(-1, w.shape[-1])
    assert w.shape[0] % k_total == 0
    tm, tn = math.gcd(tm, m), math.gcd(tn, n_out)
    if len(lhs_list) > 1:
        assert k_total <= tk
        tk = k_total
    else:
        tk = math.gcd(tk, k_total)
        k_sizes = [tk]
    nk = k_total // tk
    assert nk == 1 or (act is None and out_dtype == F32 and row_ss is None)
    n_i, n_j = m // tm, n_out // tn
    n_gain = len(norm_gains)

    def tile(i, j, k):
        return (i, j)

    def row_block(i, j, k):
        return (i, 0)

    lhs_mode = dict(pipeline_mode=pl.Buffered(1)) if single_buffer_lhs else {}
    in_specs = [pl.BlockSpec((tm, ks), lambda i, j, k: (i, k), **lhs_mode) for ks in k_sizes]
    in_specs.append(pl.BlockSpec((tk, tn), lambda i, j, k: (w_layer * nk + k, j)))
    args = list(lhs_list) + [w]
    if res is not None:
        in_specs.append(pl.BlockSpec((tm, tn), tile))
        args.append(res)
    if row_ss is not None:
        in_specs.append(pl.BlockSpec((tm, LANES), row_block))
        args.append(row_ss)
    gains = [g for g in norm_gains if g is not None]
    gain_rows = tuple(None if g is None else sum(h is not None for h in norm_gains[:n])
                      for n, g in enumerate(norm_gains))
    if gains:
        in_specs.append(pl.BlockSpec((len(gains), tn), lambda i, j, k: (0, j)))
        args.append(jnp.stack(gains).astype(F32))
    out_specs = [pl.BlockSpec((tm, tn), tile)] * (1 + n_gain)
    out_shape = [jax.ShapeDtypeStruct((m, n_out), out_dtype)]
    out_shape += [jax.ShapeDtypeStruct((m, n_out), BF16)] * n_gain
    if n_gain:
        out_specs.append(pl.BlockSpec((tm, LANES), row_block))
        out_shape.append(jax.ShapeDtypeStruct((m, LANES), F32))
    side_in, side_args, side_out, side_shapes, side_flags, finalize = _side_cast_plan(
        side_casts, n_i * n_j * nk, lambda i, j, k: (i * n_j + j) * nk + k)
    kernel = functools.partial(
        _linear_kernel, k_sizes=tuple(k_sizes), nk=nk, act=act, has_res=res is not None,
        ss_dim=k_total if row_ss is not None else 0, gain_rows=gain_rows, side_flags=side_flags)
    n_main = len(out_specs)
    outs = pl.pallas_call(
        kernel,
        grid=(n_i, n_j, nk),
        in_specs=in_specs + side_in,
        out_specs=out_specs + side_out,
        out_shape=out_shape + side_shapes,
        compiler_params=_params("parallel", "arbitrary" if n_gain else "parallel", "arbitrary"),
        name=name,
    )(*args, *side_args)
    outs = list(outs[:n_main]) + finalize(outs[n_main:])
    return outs[0] if len(outs) == 1 else tuple(outs)


def _pool_kernel(h_ref, w_ref, wpg_ref, scale_ref, *rest, tm, side_flags):
    rest = list(rest)
    side_jobs = _pop_side_inputs(rest, side_flags)
    o_ref = rest.pop(0)
    u_ref, ub_ref, p_ref, carry_ref = rest[len(side_flags):]
    _run_side_casts(side_jobs, rest[:len(side_flags)])
    i = pl.program_id(1)
    g = pl.program_id(2)
    blk = POOL_BLOCK
    half = tm // 2
    u_ref[:half, :] = jnp.dot(h_ref[:half, :], w_ref[...], preferred_element_type=F32)
    u_ref[half:, :] = jnp.dot(h_ref[half:, :], w_ref[...], preferred_element_type=F32)
    ub_ref[:blk - POOL_HALO, :] = jnp.zeros((blk - POOL_HALO, ub_ref.shape[1]), ub_ref.dtype)

    @pl.when(i == 0)
    def _():
        ub_ref[blk - POOL_HALO:blk, :] = jnp.zeros((POOL_HALO, ub_ref.shape[1]), ub_ref.dtype)

    @pl.when(i > 0)
    def _():
        ub_ref[blk - POOL_HALO:blk, :] = carry_ref[g]

    ub_ref[blk:, :] = u_ref[...].astype(ub_ref.dtype)
    carry_ref[g] = ub_ref[blk + tm - POOL_HALO:, :]

    window = jnp.left_shift(2, g)
    row = lax.broadcasted_iota(jnp.int32, (blk, 2 * blk), 0) + blk
    col = lax.broadcasted_iota(jnp.int32, (blk, 2 * blk), 1)
    band = jnp.where((col <= row) & (col > row - window), 1.0, 0.0).astype(BF16)
    pos = i * tm + lax.broadcasted_iota(jnp.int32, (tm, 1), 0)
    inv_cnt = 1.0 / jnp.minimum(pos + 1, window).astype(F32)
    for r in range(tm // blk):
        rows = slice(r * blk, (r + 1) * blk)
        s = jnp.dot(band, ub_ref[r * blk:(r + 2) * blk, :], preferred_element_type=F32)
        p_ref[rows, :] = (s * inv_cnt[rows] - u_ref[rows, :]).astype(p_ref.dtype)

    mixed = jnp.dot(p_ref[...], wpg_ref[...], preferred_element_type=F32) * scale_ref[...]
    o_ref[...] = mixed.astype(o_ref.dtype)


def _pool_branch(h, w_in, w_pg, scale, batch, seq, tm=1024, side_casts=()):
    n, d = h.shape
    groups, gw = w_pg.shape[0], w_pg.shape[1]
    assert POOL_WINDOWS == tuple(2 << g for g in range(groups)) and max(POOL_WINDOWS) <= POOL_HALO
    tm = min(tm, seq)
    n_i = seq // tm
    side_in, side_args, side_out, side_shapes, side_flags, finalize = _side_cast_plan(
        side_casts, batch * n_i * groups, lambda b, i, g: (b * n_i + i) * groups + g)
    outs = pl.pallas_call(
        functools.partial(_pool_kernel, tm=tm, side_flags=side_flags),
        grid=(batch, n_i, groups),
        in_specs=[pl.BlockSpec((tm, d), lambda b, i, g: (b * n_i + i, 0)),
                  pl.BlockSpec((d, gw), lambda b, i, g: (0, g)),
                  pl.BlockSpec((None, gw, gw), lambda b, i, g: (g, 0, 0)),
                  pl.BlockSpec((1, gw), lambda b, i, g: (0, g))] + side_in,
        out_specs=[pl.BlockSpec((tm, gw), lambda b, i, g: (b * n_i + i, g))] + side_out,
        out_shape=[jax.ShapeDtypeStruct((n, groups * gw), BF16)] + side_shapes,
        scratch_shapes=[pltpu.VMEM((tm, gw), F32), pltpu.VMEM((POOL_BLOCK + tm, gw), BF16),
                        pltpu.VMEM((tm, gw), BF16), pltpu.VMEM((groups, POOL_HALO, gw), BF16)],
        compiler_params=_params("parallel", "arbitrary", "arbitrary"),
        name="pool_branch",
    )(h, w_in, w_pg, scale.reshape(1, -1), *side_args)
    return outs[0] if not side_casts else (outs[0], *finalize(outs[1:]))


def _mem_attn_kernel(h_ref, wq_ref, k_ref, v_ref, *rest, scale, ss_dim, hd):
    rest = list(rest)
    ss_ref = rest.pop(0) if ss_dim else None
    (o_ref,) = rest
    q = jnp.dot(h_ref[...], wq_ref[...], preferred_element_type=F32)
    if ss_dim:
        q = q * _inv_rms(ss_ref, ss_dim)
    q = q.astype(BF16)
    for head in range(MEM_HEADS):
        cols = slice(head * hd, (head + 1) * hd)
        s = lax.dot_general(q[:, cols], k_ref[:, cols], (((1,), (1,)), ((), ())),
                            preferred_element_type=F32) * scale
        m = jnp.max(s, axis=-1, keepdims=True)
        p = jnp.exp(s - m)
        inv_den = 1.0 / jnp.sum(p, axis=-1, keepdims=True)
        o = jnp.dot(p.astype(BF16), v_ref[:, cols], preferred_element_type=F32) * inv_den
        o_ref[:, cols] = o.astype(o_ref.dtype)


def _mem_attn(h, w_in, q_col0, mkv, batch, seq, mem_len, tm=1024, row_ss=None):
    n, d = h.shape
    width = mkv.shape[1] // 2
    hd = width // MEM_HEADS
    tm = min(tm, seq)
    n_i = seq // tm
    assert q_col0 % width == 0
    in_specs = [pl.BlockSpec((tm, d), lambda b, i: (b * n_i + i, 0)),
                pl.BlockSpec((d, width), lambda b, i: (0, q_col0 // width)),
                pl.BlockSpec((mem_len, width), lambda b, i: (b, 0)),
                pl.BlockSpec((mem_len, width), lambda b, i: (b, 1))]
    args = [h, w_in, mkv, mkv]
    if row_ss is not None:
        in_specs.append(pl.BlockSpec((tm, LANES), lambda b, i: (b * n_i + i, 0)))
        args.append(row_ss)
    return pl.pallas_call(
        functools.partial(_mem_attn_kernel, scale=1.0 / math.sqrt(hd),
                          ss_dim=d if row_ss is not None else 0, hd=hd),
        grid=(batch, n_i),
        in_specs=in_specs,
        out_specs=pl.BlockSpec((tm, width), lambda b, i: (b * n_i + i, 0)),
        out_shape=jax.ShapeDtypeStruct((n, width), BF16),
        compiler_params=_params("parallel", "parallel"),
        name="mem_attn",
    )(*args)


def _t5_bucket(dist):
    max_exact = NUM_BUCKETS // 2
    d32 = jnp.maximum(dist, 1).astype(F32)
    large = max_exact + (jnp.log(d32 / max_exact) / math.log(MAX_DISTANCE / max_exact)
                         * (NUM_BUCKETS - max_exact)).astype(jnp.int32)
    large = jnp.minimum(large, NUM_BUCKETS - 1)
    return jnp.where(dist < max_exact, dist, large)


def _bucket_tables():
    qi = jnp.arange(ATT_BLOCK)[:, None]
    kj = jnp.arange(2 * ATT_BLOCK)[None, :]
    delta = qi + ATT_BLOCK - kj
    band = (delta >= 0) & (delta <= ATT_BLOCK)
    tabs = [jnp.where(band, _t5_bucket(jnp.maximum(delta, 0) * dil), -1) for _, dil in DIL_CONFIGS]
    return jnp.stack(tabs).astype(jnp.int32)


def _bias_table_kernel(rb_ref, bkt_ref, o_ref):
    gh = pl.program_id(0)
    bkt = bkt_ref[...]
    bias = jnp.full(bkt.shape, MASKED, F32)
    for b in range(NUM_BUCKETS):
        bias = jnp.where(bkt == b, rb_ref[b, gh], bias)
    o_ref[...] = bias


def _bias_tables(rel_bias, heads):
    n_gh = rel_bias.shape[1]
    tab = (ATT_BLOCK, 2 * ATT_BLOCK)
    return pl.pallas_call(
        _bias_table_kernel,
        grid=(n_gh,),
        in_specs=[pl.BlockSpec(memory_space=pltpu.SMEM),
                  pl.BlockSpec((None,) + tab, lambda gh: (gh // heads, 0, 0))],
        out_specs=pl.BlockSpec((None,) + tab, lambda gh: (gh, 0, 0)),
        out_shape=jax.ShapeDtypeStruct((n_gh,) + tab, F32),
        compiler_params=_params("parallel"),
        name="bias_tables",
    )(rel_bias.astype(F32), _bucket_tables())


def _dil_attn_kernel(b0, b1, b2, q0, q1, q2, k0, k1, k2, v0, v1, v2, o_ref,
                     og0, og1, og2, lg0, lg1, lg2, *, seq, scale):
    bias_refs = (b0, b1, b2)
    q_refs, k_refs, v_refs = (q0, q1, q2), (k0, k1, k2), (v0, v1, v2)
    o_nat, lse_nat = (og0, og1, og2), (lg0, lg1, lg2)
    blk = ATT_BLOCK

    def qk(q3, k3):
        return jnp.einsum("uqd,ukd->uqk", q3, k3, preferred_element_type=F32) * scale

    def pv(p3, v3):
        return jnp.einsum("uqk,ukd->uqd", p3.astype(BF16), v3, preferred_element_type=F32)

    for g, (_, dil) in enumerate(DIL_CONFIGS):
        sub_len = seq // dil
        n_blk = sub_len // blk
        units = dil * n_blk

        def blocks(ref):
            parts = []
            for r in range(dil):
                rows = pl.ds(r, sub_len, stride=dil) if dil > 1 else pl.ds(0, sub_len)
                parts.append(ref[rows, :].astype(BF16).reshape(n_blk, blk, HEAD_DIM))
            return parts[0] if dil == 1 else jnp.concatenate(parts, axis=0)

        q3, k3, v3 = blocks(q_refs[g]), blocks(k_refs[g]), blocks(v_refs[g])
        bias = bias_refs[g][...]
        s_cur = qk(q3, k3) + bias[None, :, blk:]
        m = jnp.max(s_cur, axis=-1, keepdims=True)
        if n_blk > 1:
            k_prev = jnp.concatenate([k3[:1], k3[:-1]], axis=0)
            v_prev = jnp.concatenate([v3[:1], v3[:-1]], axis=0)
            unit = lax.broadcasted_iota(jnp.int32, (units, 1, 1), 0)
            bias_prev = jnp.where(unit % n_blk == 0, MASKED, bias[None, :, :blk])
            s_prev = qk(q3, k_prev) + bias_prev
            m = jnp.maximum(m, jnp.max(s_prev, axis=-1, keepdims=True))
        p_cur = jnp.exp(s_cur - m)
        den = jnp.sum(p_cur, axis=-1, keepdims=True)
        acc = pv(p_cur, v3)
        if n_blk > 1:
            p_prev = jnp.exp(s_prev - m)
            den = den + jnp.sum(p_prev, axis=-1, keepdims=True)
            acc = acc + pv(p_prev, v_prev)
        o3 = acc * (1.0 / den)
        lse3 = m + jnp.log(den)
        for u in range(units):
            r, n = divmod(u, n_blk)
            start = r + dil * n * blk
            nat = pl.ds(start, blk, stride=dil) if dil > 1 else pl.ds(start, blk)
            o_nat[g][nat, :] = o3[u]
            lse_nat[g][nat, :] = jnp.broadcast_to(lse3[u], (blk, HEAD_DIM))

    l0, l1, l2 = lse_nat[0][...], lse_nat[1][...], lse_nat[2][...]
    top = jnp.maximum(jnp.maximum(l0, l1), l2)
    w0, w1, w2 = jnp.exp(l0 - top), jnp.exp(l1 - top), jnp.exp(l2 - top)
    inv = 1.0 / (w0 + w1 + w2)
    out = (o_nat[0][...] * w0 + o_nat[1][...] * w1 + o_nat[2][...] * w2) * inv
    o_ref[...] = out.astype(o_ref.dtype)


def _dil_attn(q, kv, rel_bias, batch, seq):
    n = q.shape[0]
    n_groups = len(DIL_CONFIGS)
    heads = rel_bias.shape[1] // n_groups
    gh = n_groups * heads
    bias = _bias_tables(rel_bias, heads)

    def col_spec(off):
        return pl.BlockSpec((seq, HEAD_DIM), lambda b, hh: (b, off + hh))

    in_specs = [pl.BlockSpec((None, ATT_BLOCK, 2 * ATT_BLOCK),
                             lambda b, hh, g=g: (g * heads + hh, 0, 0)) for g in range(n_groups)]
    in_specs += [col_spec(g * heads) for g in range(n_groups)]
    in_specs += [col_spec(g * heads) for g in range(n_groups)]
    in_specs += [col_spec(gh + g * heads) for g in range(n_groups)]
    return pl.pallas_call(
        functools.partial(_dil_attn_kernel, seq=seq, scale=1.0 / math.sqrt(HEAD_DIM)),
        grid=(batch, heads),
        in_specs=in_specs,
        out_specs=pl.BlockSpec((seq, HEAD_DIM), lambda b, hh: (b, hh)),
        out_shape=jax.ShapeDtypeStruct((n, heads * HEAD_DIM), BF16),
        scratch_shapes=[pltpu.VMEM((seq, HEAD_DIM), F32)] * (2 * n_groups),
        compiler_params=_params("parallel", "arbitrary"),
        name="dil_attn",
    )(bias, bias, bias, q, q, q, kv, kv, kv, kv, kv, kv)


def kernel(x, mem, a_norm, a_w_in, a_w_pg, a_scale, a_w_out, kv_norm, w_kv, b_norm, b_w_in,
           b_w_out, mem_norm, w_mem_kv, mlp_norm, mlp_w1, mlp_w2, rel_bias, final_norm):
    batch, seq, d = x.shape
    mem_len = mem.shape[1]
    n = batch * seq
    pool_width = a_scale.shape[-1]
    dil_q_width = w_kv.shape[1] // 2
    assert a_norm.shape[0] == 1 and b_norm.shape[0] == 1 and mlp_norm.shape[0] == 2

    x0 = x.reshape(n, d)
    (mem_h,) = _rmsnorm(mem.reshape(batch * mem_len, d), [mem_norm], BF16)
    mkv = [_linear([mem_h], (l, w_mem_kv), w_mem_kv.shape[2], BF16, tm=2048, tn=512,
                   single_buffer_lhs=True, name="mem_kv") for l in range(2)]

    (h0,) = _rmsnorm(x0, [a_norm[0]], BF16)
    a_in = a_w_in[0].astype(BF16)
    mixed, a_out_w, b_out_w, w1 = _pool_branch(
        h0, a_in, a_w_pg[0].astype(BF16), a_scale[0], batch, seq,
        side_casts=[(0, a_w_out), (0, b_w_out), (0, mlp_w1)])
    mem_out = _mem_attn(h0, a_in, pool_width, mkv[0], batch, seq, mem_len)
    x1, xg, ss = _linear([mixed, mem_out], a_out_w, d, F32, res=x0, norm_gains=[mlp_norm[0]],
                         tn=512, name="a_out")
    hid, w2, w1_next, kv_w, b_in = _linear(
        [xg], w1, mlp_w1.shape[2], BF16, act="relu2", row_ss=ss,
        side_casts=[(0, mlp_w2), (1, mlp_w1), (0, w_kv[None], kv_norm), (0, b_w_in, b_norm[0])],
        name="mlp_up")
    x2, xn, ss = _linear([hid], w2, d, F32, res=x1, norm_gains=[None], name="mlp_down")

    kv = _linear([xn], kv_w, w_kv.shape[1], F32, row_ss=ss, name="kv_proj")
    q = _linear([xn], b_in, dil_q_width, F32, row_ss=ss, name="q_proj")
    dil_out = _dil_attn(q, kv, rel_bias, batch, seq)
    mem_out = _mem_attn(xn, b_in, dil_q_width, mkv[1], batch, seq, mem_len, row_ss=ss)
    x3, xg, ss = _linear([dil_out, mem_out], b_out_w, d, F32, res=x2, norm_gains=[mlp_norm[1]],
                         name="b_out")
    hid, w2 = _linear([xg], w1_next, mlp_w1.shape[2], BF16, act="relu2", row_ss=ss,
                      side_casts=[(1, mlp_w2)], name="mlp_up")
    x4 = _linear([hid], w2, d, F32, res=x3, name="mlp_down")

    (out,) = _rmsnorm(x4, [final_norm], F32)
    return out.reshape(batch, seq, d)
```

```python
import functools
import math

import jax
import jax.numpy as jnp
from jax import lax
from jax.experimental import pallas as pl
from jax.experimental.pallas import tpu as pltpu

F32 = jnp.float32
BF16 = jnp.bfloat16

EPS = 1e-6
POOL_WINDOWS = (2, 4, 8, 16)
POOL_HALO = 16
POOL_BLOCK = 128
MEM_HEADS = 4
LANES = 128
HEAD_DIM = 128
DIL_CONFIGS = ((128, 1), (512, 4), (2048, 16))
ATT_BLOCK = 128
NUM_BUCKETS = 32
MAX_DISTANCE = 2048
MASKED = -1e30
V7X_VMEM_BYTES = 64 * 1024 * 1024
VMEM_LIMIT_BYTES = V7X_VMEM_BYTES - 1 * 1024 * 1024


def _params(*semantics):
    return pltpu.CompilerParams(dimension_semantics=semantics, vmem_limit_bytes=VMEM_LIMIT_BYTES)


def _rmsnorm_kernel(x_ref, g_ref, *o_refs):
    x = x_ref[...]
    y = x * lax.rsqrt(jnp.mean(x * x, axis=-1, keepdims=True) + EPS)
    for n, o_ref in enumerate(o_refs):
        o_ref[...] = (y * g_ref[n:n + 1, :]).astype(o_ref.dtype)


def _rmsnorm(x, gains, out_dtype, tm=512):
    m, d = x.shape
    n = len(gains)
    tm = math.gcd(tm, m)
    g = jnp.stack(gains).astype(F32)
    outs = pl.pallas_call(
        _rmsnorm_kernel,
        grid=(m // tm,),
        in_specs=[pl.BlockSpec((tm, d), lambda i: (i, 0)),
                  pl.BlockSpec((n, d), lambda i: (0, 0))],
        out_specs=[pl.BlockSpec((tm, d), lambda i: (i, 0))] * n,
        out_shape=[jax.ShapeDtypeStruct((m, d), out_dtype)] * n,
        compiler_params=_params("parallel"),
        name="rmsnorm",
    )(x, g)
    return list(outs)


def _norm_and_mem_kv_kernel(x_ref, g_ref, mem_ref, w_ref, *rest, side_flags):
    rest = list(rest)
    side_jobs = _pop_side_inputs(rest, side_flags)
    h_ref, kv_ref = rest[:2]
    _run_side_casts(side_jobs, rest[2:])
    x = x_ref[...]
    y = x * lax.rsqrt(jnp.mean(x * x, axis=-1, keepdims=True) + EPS)
    h_ref[...] = (y * g_ref[...]).astype(h_ref.dtype)
    w = w_ref[...].astype(BF16)
    kv_ref[...] = jnp.dot(mem_ref[...], w, preferred_element_type=F32).astype(kv_ref.dtype)


def _norm_and_mem_kv(x, gain, mem_h, w_mem_kv, tn=LANES, side_casts=()):
    n, d = x.shape
    layers, _, n_out = w_mem_kv.shape
    m = mem_h.shape[0]
    tiles = n_out // tn
    steps = layers * tiles
    slab = n // steps
    assert tiles * tn == n_out and slab * steps == n and slab % 16 == 0
    side_in, side_args, side_out, side_shapes, side_flags, finalize = _side_cast_plan(
        side_casts, steps, lambda s: s)
    h, kv, *sides = pl.pallas_call(
        functools.partial(_norm_and_mem_kv_kernel, side_flags=side_flags),
        grid=(steps,),
        in_specs=[pl.BlockSpec((slab, d), lambda s: (s, 0)),
                  pl.BlockSpec((1, d), lambda s: (0, 0)),
                  pl.BlockSpec((m, d), lambda s: (0, 0), pipeline_mode=pl.Buffered(1)),
                  pl.BlockSpec((d, tn), lambda s: (s // tiles, s % tiles))] + side_in,
        out_specs=[pl.BlockSpec((slab, d), lambda s: (s, 0)),
                   pl.BlockSpec((None, m, tn), lambda s: (s // tiles, 0, s % tiles))] + side_out,
        out_shape=[jax.ShapeDtypeStruct((n, d), BF16),
                   jax.ShapeDtypeStruct((layers, m, n_out), BF16)] + side_shapes,
        compiler_params=_params("parallel"),
        name="norm_and_mem_kv",
    )(x, gain.reshape(1, d).astype(F32), mem_h, w_mem_kv.reshape(layers * d, n_out), *side_args)
    return h, [kv[l] for l in range(layers)], finalize(sides)


def _side_cast_plan(side_casts, steps, step_index):
    in_specs, args, out_specs, out_shapes, shapes, flags = [], [], [], [], [], []
    for layer, stack, *gain in side_casts:
        _, rows, cols = stack.shape
        fold = 1
        while (rows * fold) % (steps * 16):
            fold *= 2
        assert fold == 1 or (cols % (fold * LANES) == 0 and not gain)
        r, c = rows * fold, cols // fold
        slab = r // steps
        in_specs.append(pl.BlockSpec(
            (slab, c), lambda *g, layer=layer: (layer * steps + step_index(*g), 0)))
        args.append(stack.reshape(-1, c))
        if gain:
            in_specs.append(pl.BlockSpec((slab, LANES), lambda *g: (step_index(*g), 0)))
            args.append(jnp.broadcast_to(gain[0].astype(F32)[:, None], (rows, LANES)))
        out_specs.append(pl.BlockSpec((slab, c), lambda *g: (step_index(*g), 0)))
        out_shapes.append(jax.ShapeDtypeStruct((r, c), BF16))
        shapes.append((rows, cols))
        flags.append(bool(gain))

    def finalize(outs):
        return [o.reshape(s) for o, s in zip(outs, shapes)]

    return in_specs, args, out_specs, out_shapes, tuple(flags), finalize


def _pop_side_inputs(refs, side_flags):
    jobs = []
    for has_gain in side_flags:
        src_ref = refs.pop(0)
        jobs.append((src_ref, refs.pop(0) if has_gain else None))
    return jobs


def _run_side_casts(jobs, dst_refs):
    for (src_ref, gain_ref), dst_ref in zip(jobs, dst_refs):
        if gain_ref is None:
            dst_ref[...] = src_ref[...].astype(dst_ref.dtype)
            continue
        g = gain_ref[...]
        for c in range(0, src_ref.shape[1], LANES):
            dst_ref[:, c:c + LANES] = (src_ref[:, c:c + LANES] * g).astype(dst_ref.dtype)


def _inv_rms(ss_ref, dim):
    return lax.rsqrt(ss_ref[:, 0:1] * (1.0 / dim) + EPS)


def _linear_kernel(*refs, k_sizes, nk, act, has_res, ss_dim, gain_rows, side_flags):
    refs = list(refs)
    lhs_refs = [refs.pop(0) for _ in k_sizes]
    w_ref = refs.pop(0)
    res_ref = refs.pop(0) if has_res else None
    in_ss_ref = refs.pop(0) if ss_dim else None
    gain_ref = refs.pop(0) if any(r is not None for r in gain_rows) else None
    side_jobs = _pop_side_inputs(refs, side_flags)
    o_ref = refs.pop(0)
    scaled_refs = [refs.pop(0) for _ in gain_rows]
    out_ss_ref = refs.pop(0) if gain_rows else None
    n_gain = len(gain_rows)
    _run_side_casts(side_jobs, refs)
    j = pl.program_id(1)

    def product():
        y, off = None, 0
        for lhs_ref, ks in zip(lhs_refs, k_sizes):
            w = w_ref[off:off + ks, :].astype(BF16)
            t = jnp.dot(lhs_ref[...], w, preferred_element_type=F32)
            y = t if y is None else y + t
            off += ks
        return y

    def emit_norm_inputs(y):
        for row, s_ref in zip(gain_rows, scaled_refs):
            scaled = y if row is None else y * gain_ref[row:row + 1, :]
            s_ref[...] = scaled.astype(s_ref.dtype)
        part = jnp.broadcast_to(jnp.sum(y * y, axis=-1, keepdims=True), out_ss_ref.shape)

        @pl.when(j == 0)
        def _():
            out_ss_ref[...] = part

        @pl.when(j > 0)
        def _():
            out_ss_ref[...] += part

    if nk == 1:
        y = product()
        if ss_dim:
            y = y * _inv_rms(in_ss_ref, ss_dim)
        if act == "relu2":
            y = jnp.maximum(y, 0.0)
            y = y * y
        if has_res:
            y = res_ref[...] + y
        o_ref[...] = y.astype(o_ref.dtype)
        if n_gain:
            emit_norm_inputs(y)
        return

    k = pl.program_id(2)

    @pl.when(k == 0)
    def _():
        o_ref[...] = res_ref[...] if has_res else jnp.zeros_like(o_ref)

    if not n_gain:
        o_ref[...] += product()
        return

    @pl.when(k < nk - 1)
    def _():
        o_ref[...] += product()

    @pl.when(k == nk - 1)
    def _():
        y = o_ref[...] + product()
        o_ref[...] = y
        emit_norm_inputs(y)


def _linear(lhs_list, w, n_out, out_dtype, *, res=None, act=None, row_ss=None, norm_gains=(),
            tm=1024, tn=1024, tk=4096, single_buffer_lhs=False, side_casts=(), name="linear"):
    m = lhs_list[0].shape[0]
    k_sizes = [a.shape[1] for a in lhs_list]
    k_total = sum(k_sizes)
    w_layer = 0
    if isinstance(w, tuple):
        w_layer, w = w
        w = w.reshape(-1, w.shape[-1])
    assert w.shape[0] % k_total == 0
    tm, tn = math.gcd(tm, m), math.gcd(tn, n_out)
    if len(lhs_list) > 1:
        assert k_total <= tk
        tk = k_total
    else:
        tk = math.gcd(tk, k_total)
        k_sizes = [tk]
    nk = k_total // tk
    assert nk == 1 or (act is None and out_dtype == F32 and row_ss is None)
    n_i, n_j = m // tm, n_out // tn
    n_gain = len(norm_gains)

    def tile(i, j, k):
        return (i, j)

    def row_block(i, j, k):
        return (i, 0)

    lhs_mode = dict(pipeline_mode=pl.Buffered(1)) if single_buffer_lhs else {}
    in_specs = [pl.BlockSpec((tm, ks), lambda i, j, k: (i, k), **lhs_mode) for ks in k_sizes]
    in_specs.append(pl.BlockSpec((tk, tn), lambda i, j, k: (w_layer * nk + k, j)))
    args = list(lhs_list) + [w]
    if res is not None:
        in_specs.append(pl.BlockSpec((tm, tn), tile))
        args.append(res)
    if row_ss is not None:
        in_specs.append(pl.BlockSpec((tm, LANES), row_block))
        args.append(row_ss)
    gains = [g for g in norm_gains if g is not None]
    gain_rows = tuple(None if g is None else sum(h is not None for h in norm_gains[:n])
                      for n, g in enumerate(norm_gains))
    if gains:
        in_specs.append(pl.BlockSpec((len(gains), tn), lambda i, j, k: (0, j)))
        args.append(jnp.stack(gains).astype(F32))
    out_specs = [pl.BlockSpec((tm, tn), tile)] * (1 + n_gain)
    out_shape = [jax.ShapeDtypeStruct((m, n_out), out_dtype)]
    out_shape += [jax.ShapeDtypeStruct((m, n_out), BF16)] * n_gain
    if n_gain:
        out_specs.append(pl.BlockSpec((tm, LANES), row_block))
        out_shape.append(jax.ShapeDtypeStruct((m, LANES), F32))
    side_in, side_args, side_out, side_shapes, side_flags, finalize = _side_cast_plan(
        side_casts, n_i * n_j * nk, lambda i, j, k: (i * n_j + j) * nk + k)
    kernel = functools.partial(
        _linear_kernel, k_sizes=tuple(k_sizes), nk=nk, act=act, has_res=res is not None,
        ss_dim=k_total if row_ss is not None else 0, gain_rows=gain_rows, side_flags=side_flags)
    n_main = len(out_specs)
    outs = pl.pallas_call(
        kernel,
        grid=(n_i, n_j, nk),
        in_specs=in_specs + side_in,
        out_specs=out_specs + side_out,
        out_shape=out_shape + side_shapes,
        compiler_params=_params("parallel", "arbitrary" if n_gain else "parallel", "arbitrary"),
        name=name,
    )(*args, *side_args)
    outs = list(outs[:n_main]) + finalize(outs[n_main:])
    return outs[0] if len(outs) == 1 else tuple(outs)


def _pool_kernel(h_ref, w_ref, wpg_ref, scale_ref, *rest, tm, side_flags):
    rest = list(rest)
    side_jobs = _pop_side_inputs(rest, side_flags)
    o_ref = rest.pop(0)
    u_ref, ub_ref, p_ref, carry_ref = rest[len(side_flags):]
    _run_side_casts(side_jobs, rest[:len(side_flags)])
    i = pl.program_id(1)
    g = pl.program_id(2)
    blk = POOL_BLOCK
    half = tm // 2
    u_ref[:half, :] = jnp.dot(h_ref[:half, :], w_ref[...], preferred_element_type=F32)
    u_ref[half:, :] = jnp.dot(h_ref[half:, :], w_ref[...], preferred_element_type=F32)
    ub_ref[:blk - POOL_HALO, :] = jnp.zeros((blk - POOL_HALO, ub_ref.shape[1]), ub_ref.dtype)

    @pl.when(i == 0)
    def _():
        ub_ref[blk - POOL_HALO:blk, :] = jnp.zeros((POOL_HALO, ub_ref.shape[1]), ub_ref.dtype)

    @pl.when(i > 0)
    def _():
        ub_ref[blk - POOL_HALO:blk, :] = carry_ref[g]

    ub_ref[blk:, :] = u_ref[...].astype(ub_ref.dtype)
    carry_ref[g] = ub_ref[blk + tm - POOL_HALO:, :]

    window = jnp.left_shift(2, g)
    row = lax.broadcasted_iota(jnp.int32, (blk, 2 * blk), 0) + blk
    col = lax.broadcasted_iota(jnp.int32, (blk, 2 * blk), 1)
    band = jnp.where((col <= row) & (col > row - window), 1.0, 0.0).astype(BF16)
    pos = i * tm + lax.broadcasted_iota(jnp.int32, (tm, 1), 0)
    inv_cnt = 1.0 / jnp.minimum(pos + 1, window).astype(F32)
    for r in range(tm // blk):
        rows = slice(r * blk, (r + 1) * blk)
        s = jnp.dot(band, ub_ref[r * blk:(r + 2) * blk, :], preferred_element_type=F32)
        p_ref[rows, :] = (s * inv_cnt[rows] - u_ref[rows, :]).astype(p_ref.dtype)

    mixed = jnp.dot(p_ref[...], wpg_ref[...], preferred_element_type=F32) * scale_ref[...]
    o_ref[...] = mixed.astype(o_ref.dtype)


def _pool_branch(h, w_in, w_pg, scale, batch, seq, tm=1024, side_casts=()):
    n, d = h.shape
    groups, gw = w_pg.shape[0], w_pg.shape[1]
    assert POOL_WINDOWS == tuple(2 << g for g in range(groups)) and max(POOL_WINDOWS) <= POOL_HALO
    tm = min(tm, seq)
    n_i = seq // tm
    side_in, side_args, side_out, side_shapes, side_flags, finalize = _side_cast_plan(
        side_casts, batch * n_i * groups, lambda b, i, g: (b * n_i + i) * groups + g)
    outs = pl.pallas_call(
        functools.partial(_pool_kernel, tm=tm, side_flags=side_flags),
        grid=(batch, n_i, groups),
        in_specs=[pl.BlockSpec((tm, d), lambda b, i, g: (b * n_i + i, 0)),
                  pl.BlockSpec((d, gw), lambda b, i, g: (0, g)),
                  pl.BlockSpec((None, gw, gw), lambda b, i, g: (g, 0, 0)),
                  pl.BlockSpec((1, gw), lambda b, i, g: (0, g))] + side_in,
        out_specs=[pl.BlockSpec((tm, gw), lambda b, i, g: (b * n_i + i, g))] + side_out,
        out_shape=[jax.ShapeDtypeStruct((n, groups * gw), BF16)] + side_shapes,
        scratch_shapes=[pltpu.VMEM((tm, gw), F32), pltpu.VMEM((POOL_BLOCK + tm, gw), BF16),
                        pltpu.VMEM((tm, gw), BF16), pltpu.VMEM((groups, POOL_HALO, gw), BF16)],
        compiler_params=_params("parallel", "arbitrary", "arbitrary"),
        name="pool_branch",
    )(h, w_in, w_pg, scale.reshape(1, -1), *side_args)
    return outs[0] if not side_casts else (outs[0], *finalize(outs[1:]))


def _mem_attn_kernel(h_ref, wq_ref, k_ref, v_ref, *rest, scale, ss_dim, hd):
    rest = list(rest)
    ss_ref = rest.pop(0) if ss_dim else None
    (o_ref,) = rest
    q = jnp.dot(h_ref[...], wq_ref[...], preferred_element_type=F32)
    if ss_dim:
        q = q * _inv_rms(ss_ref, ss_dim)
    q = q.astype(BF16)
    for head in range(MEM_HEADS):
        cols = slice(head * hd, (head + 1) * hd)
        s = lax.dot_general(q[:, cols], k_ref[:, cols], (((1,), (1,)), ((), ())),
                            preferred_element_type=F32) * scale
        m = jnp.max(s, axis=-1, keepdims=True)
        p = jnp.exp(s - m)
        inv_den = 1.0 / jnp.sum(p, axis=-1, keepdims=True)
        o = jnp.dot(p.astype(BF16), v_ref[:, cols], preferred_element_type=F32) * inv_den
        o_ref[:, cols] = o.astype(o_ref.dtype)


def _mem_attn(h, w_in, q_col0, mkv, batch, seq, mem_len, tm=1024, row_ss=None):
    n, d = h.shape
    width = mkv.shape[1] // 2
    hd = width // MEM_HEADS
    tm = min(tm, seq)
    n_i = seq // tm
    assert q_col0 % width == 0
    in_specs = [pl.BlockSpec((tm, d), lambda b, i: (b * n_i + i, 0)),
                pl.BlockSpec((d, width), lambda b, i: (0, q_col0 // width)),
                pl.BlockSpec((mem_len, width), lambda b, i: (b, 0)),
                pl.BlockSpec((mem_len, width), lambda b, i: (b, 1))]
    args = [h, w_in, mkv, mkv]
    if row_ss is not None:
        in_specs.append(pl.BlockSpec((tm, LANES), lambda b, i: (b * n_i + i, 0)))
        args.append(row_ss)
    return pl.pallas_call(
        functools.partial(_mem_attn_kernel, scale=1.0 / math.sqrt(hd),
                          ss_dim=d if row_ss is not None else 0, hd=hd),
        grid=(batch, n_i),
        in_specs=in_specs,
        out_specs=pl.BlockSpec((tm, width), lambda b, i: (b * n_i + i, 0)),
        out_shape=jax.ShapeDtypeStruct((n, width), BF16),
        compiler_params=_params("parallel", "parallel"),
        name="mem_attn",
    )(*args)


def _t5_bucket(dist):
    max_exact = NUM_BUCKETS // 2
    d32 = jnp.maximum(dist, 1).astype(F32)
    large = max_exact + (jnp.log(d32 / max_exact) / math.log(MAX_DISTANCE / max_exact)
                         * (NUM_BUCKETS - max_exact)).astype(jnp.int32)
    large = jnp.minimum(large, NUM_BUCKETS - 1)
    return jnp.where(dist < max_exact, dist, large)


def _bucket_tables():
    qi = jnp.arange(ATT_BLOCK)[:, None]
    kj = jnp.arange(2 * ATT_BLOCK)[None, :]
    delta = qi + ATT_BLOCK - kj
    band = (delta >= 0) & (delta <= ATT_BLOCK)
    tabs = [jnp.where(band, _t5_bucket(jnp.maximum(delta, 0) * dil), -1) for _, dil in DIL_CONFIGS]
    return jnp.stack(tabs).astype(jnp.int32)


def _bias_table_kernel(rb_ref, bkt_ref, o_ref):
    gh = pl.program_id(0)
    bkt = bkt_ref[...]
    bias = jnp.full(bkt.shape, MASKED, F32)
    for b in range(NUM_BUCKETS):
        bias = jnp.where(bkt == b, rb_ref[b, gh], bias)
    o_ref[...] = bias


def _bias_tables(rel_bias, heads):
    n_gh = rel_bias.shape[1]
    tab = (ATT_BLOCK, 2 * ATT_BLOCK)
    return pl.pallas_call(
        _bias_table_kernel,
        grid=(n_gh,),
        in_specs=[pl.BlockSpec(memory_space=pltpu.SMEM),
                  pl.BlockSpec((None,) + tab, lambda gh: (gh // heads, 0, 0))],
        out_specs=pl.BlockSpec((None,) + tab, lambda gh: (gh, 0, 0)),
        out_shape=jax.ShapeDtypeStruct((n_gh,) + tab, F32),
        compiler_params=_params("parallel"),
        name="bias_tables",
    )(rel_bias.astype(F32), _bucket_tables())


def _dil_attn_kernel(b0, b1, b2, q0, q1, q2, k0, k1, k2, v0, v1, v2, o_ref,
                     og0, og1, og2, lg0, lg1, lg2, *, seq, scale):
    bias_refs = (b0, b1, b2)
    q_refs, k_refs, v_refs = (q0, q1, q2), (k0, k1, k2), (v0, v1, v2)
    o_nat, lse_nat = (og0, og1, og2), (lg0, lg1, lg2)
    blk = ATT_BLOCK

    def qk(q3, k3):
        return jnp.einsum("uqd,ukd->uqk", q3, k3, preferred_element_type=F32) * scale

    def pv(p3, v3):
        return jnp.einsum("uqk,ukd->uqd", p3.astype(BF16), v3, preferred_element_type=F32)

    for g, (_, dil) in enumerate(DIL_CONFIGS):
        sub_len = seq // dil
        n_blk = sub_len // blk
        units = dil * n_blk

        def blocks(ref):
            parts = []
            for r in range(dil):
                rows = pl.ds(r, sub_len, stride=dil) if dil > 1 else pl.ds(0, sub_len)
                parts.append(ref[rows, :].astype(BF16).reshape(n_blk, blk, HEAD_DIM))
            return parts[0] if dil == 1 else jnp.concatenate(parts, axis=0)

        q3, k3, v3 = blocks(q_refs[g]), blocks(k_refs[g]), blocks(v_refs[g])
        bias = bias_refs[g][...]
        s_cur = qk(q3, k3) + bias[None, :, blk:]
        m = jnp.max(s_cur, axis=-1, keepdims=True)
        if n_blk > 1:
            k_prev = jnp.concatenate([k3[:1], k3[:-1]], axis=0)
            v_prev = jnp.concatenate([v3[:1], v3[:-1]], axis=0)
            unit = lax.broadcasted_iota(jnp.int32, (units, 1, 1), 0)
            bias_prev = jnp.where(unit % n_blk == 0, MASKED, bias[None, :, :blk])
            s_prev = qk(q3, k_prev) + bias_prev
            m = jnp.maximum(m, jnp.max(s_prev, axis=-1, keepdims=True))
        p_cur = jnp.exp(s_cur - m)
        den = jnp.sum(p_cur, axis=-1, keepdims=True)
        acc = pv(p_cur, v3)
        if n_blk > 1:
            p_prev = jnp.exp(s_prev - m)
            den = den + jnp.sum(p_prev, axis=-1, keepdims=True)
            acc = acc + pv(p_prev, v_prev)
        o3 = acc * (1.0 / den)
        lse3 = m + jnp.log(den)
        for u in range(units):
            r, n = divmod(u, n_blk)
            start = r + dil * n * blk
            nat = pl.ds(start, blk, stride=dil) if dil > 1 else pl.ds(start, blk)
            o_nat[g][nat, :] = o3[u]
            lse_nat[g][nat, :] = jnp.broadcast_to(lse3[u], (blk, HEAD_DIM))

    l0, l1, l2 = lse_nat[0][...], lse_nat[1][...], lse_nat[2][...]
    top = jnp.maximum(jnp.maximum(l0, l1), l2)
    w0, w1, w2 = jnp.exp(l0 - top), jnp.exp(l1 - top), jnp.exp(l2 - top)
    inv = 1.0 / (w0 + w1 + w2)
    out = (o_nat[0][...] * w0 + o_nat[1][...] * w1 + o_nat[2][...] * w2) * inv
    o_ref[...] = out.astype(o_ref.dtype)


def _dil_attn(q, kv, rel_bias, batch, seq):
    n = q.shape[0]
    n_groups = len(DIL_CONFIGS)
    heads = rel_bias.shape[1] // n_groups
    gh = n_groups * heads
    bias = _bias_tables(rel_bias, heads)

    def col_spec(off):
        return pl.BlockSpec((seq, HEAD_DIM), lambda b, hh: (b, off + hh))

    in_specs = [pl.BlockSpec((None, ATT_BLOCK, 2 * ATT_BLOCK),
                             lambda b, hh, g=g: (g * heads + hh, 0, 0)) for g in range(n_groups)]
    in_specs += [col_spec(g * heads) for g in range(n_groups)]
    in_specs += [col_spec(g * heads) for g in range(n_groups)]
    in_specs += [col_spec(gh + g * heads) for g in range(n_groups)]
    return pl.pallas_call(
        functools.partial(_dil_attn_kernel, seq=seq, scale=1.0 / math.sqrt(HEAD_DIM)),
        grid=(batch, heads),
        in_specs=in_specs,
        out_specs=pl.BlockSpec((seq, HEAD_DIM), lambda b, hh: (b, hh)),
        out_shape=jax.ShapeDtypeStruct((n, heads * HEAD_DIM), BF16),
        scratch_shapes=[pltpu.VMEM((seq, HEAD_DIM), F32)] * (2 * n_groups),
        compiler_params=_params("parallel", "arbitrary"),
        name="dil_attn",
    )(bias, bias, bias, q, q, q, kv, kv, kv, kv, kv, kv)


def kernel(x, mem, a_norm, a_w_in, a_w_pg, a_scale, a_w_out, kv_norm, w_kv, b_norm, b_w_in,
           b_w_out, mem_norm, w_mem_kv, mlp_norm, mlp_w1, mlp_w2, rel_bias, final_norm):
    batch, seq, d = x.shape
    mem_len = mem.shape[1]
    n = batch * seq
    pool_width = a_scale.shape[-1]
    dil_q_width = w_kv.shape[1] // 2
    assert a_norm.shape[0] == 1 and b_norm.shape[0] == 1 and mlp_norm.shape[0] == 2

    x0 = x.reshape(n, d)
    (mem_h,) = _rmsnorm(mem.reshape(batch * mem_len, d), [mem_norm], BF16)
    groups, gw = a_w_pg.shape[1], a_w_pg.shape[2]
    h0, mkv, (a_in, a_pg) = _norm_and_mem_kv(
        x0, a_norm[0], mem_h, w_mem_kv,
        side_casts=[(0, a_w_in), (0, a_w_pg.reshape(1, groups * gw, gw))])

    mixed, a_out_w, b_out_w, w1 = _pool_branch(
        h0, a_in, a_pg.reshape(groups, gw, gw), a_scale[0], batch, seq,
        side_casts=[(0, a_w_out), (0, b_w_out), (0, mlp_w1)])
    mem_out = _mem_attn(h0, a_in, pool_width, mkv[0], batch, seq, mem_len)
    x1, xg, ss = _linear([mixed, mem_out], a_out_w, d, F32, res=x0, norm_gains=[mlp_norm[0]],
                         name="a_out")
    hid, w2, w1_next, kv_w, b_in = _linear(
        [xg], w1, mlp_w1.shape[2], BF16, act="relu2", row_ss=ss,
        side_casts=[(0, mlp_w2), (1, mlp_w1), (0, w_kv[None], kv_norm), (0, b_w_in, b_norm[0])],
        name="mlp_up")
    x2, xn, ss = _linear([hid], w2, d, F32, res=x1, norm_gains=[None], name="mlp_down")

    kv = _linear([xn], kv_w, w_kv.shape[1], F32, row_ss=ss, name="kv_proj")
    q = _linear([xn], b_in, dil_q_width, F32, row_ss=ss, name="q_proj")
    dil_out = _dil_attn(q, kv, rel_bias, batch, seq)
    mem_out = _mem_attn(xn, b_in, dil_q_width, mkv[1], batch, seq, mem_len, row_ss=ss)
    x3, xg, ss = _linear([dil_out, mem_out], b_out_w, d, F32, res=x2, norm_gains=[mlp_norm[1]],
                         name="b_out")
    hid, w2 = _linear([xg], w1_next, mlp_w1.shape[2], BF16, act="relu2", row_ss=ss,
                      side_casts=[(1, mlp_w2)], name="mlp_up")
    x4 = _linear([hid], w2, d, F32, res=x3, name="mlp_down")

    (out,) = _rmsnorm(x4, [final_norm], F32)
    return out.reshape(batch, seq, d)
```

```python
import functools
import math

import jax
import jax.numpy as jnp
from jax import lax
from jax.experimental import pallas as pl
from jax.experimental.pallas import tpu as pltpu

F32 = jnp.float32
BF16 = jnp.bfloat16

EPS = 1e-6
POOL_WINDOWS = (2, 4, 8, 16)
POOL_HALO = 16
POOL_BLOCK = 128
MEM_HEADS = 4
LANES = 128
HEAD_DIM = 128
DIL_CONFIGS = ((128, 1), (512, 4), (2048, 16))
ATT_BLOCK = 128
NUM_BUCKETS = 32
MAX_DISTANCE = 2048
MASKED = -1e30
V7X_VMEM_BYTES = 64 * 1024 * 1024
VMEM_LIMIT_BYTES = V7X_VMEM_BYTES - 1 * 1024 * 1024


def _params(*semantics):
    return pltpu.CompilerParams(dimension_semantics=semantics, vmem_limit_bytes=VMEM_LIMIT_BYTES)


def _rmsnorm_kernel(x_ref, g_ref, *o_refs):
    x = x_ref[...]
    y = x * lax.rsqrt(jnp.mean(x * x, axis=-1, keepdims=True) + EPS)
    for n, o_ref in enumerate(o_refs):
        o_ref[...] = (y * g_ref[n:n + 1, :]).astype(o_ref.dtype)


def _rmsnorm(x, gains, out_dtype, tm=512):
    m, d = x.shape
    n = len(gains)
    tm = math.gcd(tm, m)
    g = jnp.stack(gains).astype(F32)
    outs = pl.pallas_call(
        _rmsnorm_kernel,
        grid=(m // tm,),
        in_specs=[pl.BlockSpec((tm, d), lambda i: (i, 0)),
                  pl.BlockSpec((n, d), lambda i: (0, 0))],
        out_specs=[pl.BlockSpec((tm, d), lambda i: (i, 0))] * n,
        out_shape=[jax.ShapeDtypeStruct((m, d), out_dtype)] * n,
        compiler_params=_params("parallel"),
        name="rmsnorm",
    )(x, g)
    return list(outs)


def _norm_and_mem_kv_kernel(x_ref, g_ref, mem_ref, w_ref, *rest, side_flags):
    rest = list(rest)
    side_jobs = _pop_side_inputs(rest, side_flags)
    h_ref, kv_ref = rest[:2]
    _run_side_casts(side_jobs, rest[2:])
    x = x_ref[...]
    y = x * lax.rsqrt(jnp.mean(x * x, axis=-1, keepdims=True) + EPS)
    h_ref[...] = (y * g_ref[...]).astype(h_ref.dtype)
    w = w_ref[...].astype(BF16)
    kv_ref[...] = jnp.dot(mem_ref[...], w, preferred_element_type=F32).astype(kv_ref.dtype)


def _norm_and_mem_kv(x, gain, mem_h, w_mem_kv, tn=256, row_split=2, side_casts=()):
    n, d = x.shape
    layers, _, n_out = w_mem_kv.shape
    m = mem_h.shape[0]
    tm = m // row_split
    col_tiles = n_out // tn
    per_layer = row_split * col_tiles
    steps = layers * per_layer
    slab = n // steps
    assert col_tiles * tn == n_out and tm * row_split == m and slab * steps == n and slab % 16 == 0

    def part(s):
        return (s % per_layer) // col_tiles

    def col(s):
        return s % col_tiles

    side_in, side_args, side_out, side_shapes, side_flags, finalize = _side_cast_plan(
        side_casts, steps, lambda s: s)
    h, kv, *sides = pl.pallas_call(
        functools.partial(_norm_and_mem_kv_kernel, side_flags=side_flags),
        grid=(steps,),
        in_specs=[pl.BlockSpec((slab, d), lambda s: (s, 0)),
                  pl.BlockSpec((1, d), lambda s: (0, 0)),
                  pl.BlockSpec((tm, d), lambda s: (part(s), 0)),
                  pl.BlockSpec((d, tn), lambda s: (s // per_layer, col(s)))] + side_in,
        out_specs=[pl.BlockSpec((slab, d), lambda s: (s, 0)),
                   pl.BlockSpec((None, tm, tn),
                                lambda s: (s // per_layer, part(s), col(s)))] + side_out,
        out_shape=[jax.ShapeDtypeStruct((n, d), BF16),
                   jax.ShapeDtypeStruct((layers, m, n_out), BF16)] + side_shapes,
        compiler_params=_params("parallel"),
        name="norm_and_mem_kv",
    )(x, gain.reshape(1, d).astype(F32), mem_h, w_mem_kv.reshape(layers * d, n_out), *side_args)
    return h, [kv[l] for l in range(layers)], finalize(sides)


def _side_cast_plan(side_casts, steps, step_index):
    in_specs, args, out_specs, out_shapes, shapes, flags = [], [], [], [], [], []
    for layer, stack, *gain in side_casts:
        _, rows, cols = stack.shape
        fold = 1
        while (rows * fold) % (steps * 16):
            fold *= 2
        assert fold == 1 or (cols % (fold * LANES) == 0 and not gain)
        r, c = rows * fold, cols // fold
        slab = r // steps
        in_specs.append(pl.BlockSpec(
            (slab, c), lambda *g, layer=layer: (layer * steps + step_index(*g), 0)))
        args.append(stack.reshape(-1, c))
        if gain:
            in_specs.append(pl.BlockSpec((slab, LANES), lambda *g: (step_index(*g), 0)))
            args.append(jnp.broadcast_to(gain[0].astype(F32)[:, None], (rows, LANES)))
        out_specs.append(pl.BlockSpec((slab, c), lambda *g: (step_index(*g), 0)))
        out_shapes.append(jax.ShapeDtypeStruct((r, c), BF16))
        shapes.append((rows, cols))
        flags.append(bool(gain))

    def finalize(outs):
        return [o.reshape(s) for o, s in zip(outs, shapes)]

    return in_specs, args, out_specs, out_shapes, tuple(flags), finalize


def _pop_side_inputs(refs, side_flags):
    jobs = []
    for has_gain in side_flags:
        src_ref = refs.pop(0)
        jobs.append((src_ref, refs.pop(0) if has_gain else None))
    return jobs


def _run_side_casts(jobs, dst_refs):
    for (src_ref, gain_ref), dst_ref in zip(jobs, dst_refs):
        if gain_ref is None:
            dst_ref[...] = src_ref[...].astype(dst_ref.dtype)
            continue
        g = gain_ref[...]
        for c in range(0, src_ref.shape[1], LANES):
            dst_ref[:, c:c + LANES] = (src_ref[:, c:c + LANES] * g).astype(dst_ref.dtype)


def _inv_rms(ss_ref, dim):
    return lax.rsqrt(ss_ref[:, 0:1] * (1.0 / dim) + EPS)


def _linear_kernel(*refs, k_sizes, nk, act, has_res, ss_dim, gain_rows, side_flags):
    refs = list(refs)
    lhs_refs = [refs.pop(0) for _ in k_sizes]
    w_ref = refs.pop(0)
    res_ref = refs.pop(0) if has_res else None
    in_ss_ref = refs.pop(0) if ss_dim else None
    gain_ref = refs.pop(0) if any(r is not None for r in gain_rows) else None
    side_jobs = _pop_side_inputs(refs, side_flags)
    o_ref = refs.pop(0)
    scaled_refs = [refs.pop(0) for _ in gain_rows]
    out_ss_ref = refs.pop(0) if gain_rows else None
    n_gain = len(gain_rows)
    _run_side_casts(side_jobs, refs)
    j = pl.program_id(1)

    def product():
        y, off = None, 0
        for lhs_ref, ks in zip(lhs_refs, k_sizes):
            t = jnp.dot(lhs_ref[...], w_ref[off:off + ks, :], preferred_element_type=F32)
            y = t if y is None else y + t
            off += ks
        return y

    def emit_norm_inputs(y):
        for row, s_ref in zip(gain_rows, scaled_refs):
            scaled = y if row is None else y * gain_ref[row:row + 1, :]
            s_ref[...] = scaled.astype(s_ref.dtype)
        part = jnp.broadcast_to(jnp.sum(y * y, axis=-1, keepdims=True), out_ss_ref.shape)

        @pl.when(j == 0)
        def _():
            out_ss_ref[...] = part

        @pl.when(j > 0)
        def _():
            out_ss_ref[...] += part

    if nk == 1:
        y = product()
        if ss_dim:
            y = y * _inv_rms(in_ss_ref, ss_dim)
        if act == "relu2":
            y = jnp.maximum(y, 0.0)
            y = y * y
        if has_res:
            y = res_ref[...] + y
        o_ref[...] = y.astype(o_ref.dtype)
        if n_gain:
            emit_norm_inputs(y)
        return

    k = pl.program_id(2)

    @pl.when(k == 0)
    def _():
        o_ref[...] = res_ref[...] if has_res else jnp.zeros_like(o_ref)

    if not n_gain:
        o_ref[...] += product()
        return

    @pl.when(k < nk - 1)
    def _():
        o_ref[...] += product()

    @pl.when(k == nk - 1)
    def _():
        y = o_ref[...] + product()
        o_ref[...] = y
        emit_norm_inputs(y)


def _linear(lhs_list, w, n_out, out_dtype, *, res=None, act=None, row_ss=None, norm_gains=(),
            tm=1024, tn=1024, tk=4096, side_casts=(), name="linear"):
    m = lhs_list[0].shape[0]
    k_sizes = [a.shape[1] for a in lhs_list]
    k_total = sum(k_sizes)
    assert w.shape[0] == k_total
    tm, tn = math.gcd(tm, m), math.gcd(tn, n_out)
    if len(lhs_list) > 1:
        assert k_total <= tk
        tk = k_total
    else:
        tk = math.gcd(tk, k_total)
        k_sizes = [tk]
    nk = k_total // tk
    assert nk == 1 or (act is None and out_dtype == F32 and row_ss is None)
    n_i, n_j = m // tm, n_out // tn
    n_gain = len(norm_gains)

    def tile(i, j, k):
        return (i, j)

    def row_block(i, j, k):
        return (i, 0)

    in_specs = [pl.BlockSpec((tm, ks), lambda i, j, k: (i, k)) for ks in k_sizes]
    in_specs.append(pl.BlockSpec((tk, tn), lambda i, j, k: (k, j)))
    args = list(lhs_list) + [w]
    if res is not None:
        in_specs.append(pl.BlockSpec((tm, tn), tile))
        args.append(res)
    if row_ss is not None:
        in_specs.append(pl.BlockSpec((tm, LANES), row_block))
        args.append(row_ss)
    gains = [g for g in norm_gains if g is not None]
    gain_rows = tuple(None if g is None else sum(h is not None for h in norm_gains[:n])
                      for n, g in enumerate(norm_gains))
    if gains:
        in_specs.append(pl.BlockSpec((len(gains), tn), lambda i, j, k: (0, j)))
        args.append(jnp.stack(gains).astype(F32))
    out_specs = [pl.BlockSpec((tm, tn), tile)] * (1 + n_gain)
    out_shape = [jax.ShapeDtypeStruct((m, n_out), out_dtype)]
    out_shape += [jax.ShapeDtypeStruct((m, n_out), BF16)] * n_gain
    if n_gain:
        out_specs.append(pl.BlockSpec((tm, LANES), row_block))
        out_shape.append(jax.ShapeDtypeStruct((m, LANES), F32))
    side_in, side_args, side_out, side_shapes, side_flags, finalize = _side_cast_plan(
        side_casts, n_i * n_j * nk, lambda i, j, k: (i * n_j + j) * nk + k)
    kernel = functools.partial(
        _linear_kernel, k_sizes=tuple(k_sizes), nk=nk, act=act, has_res=res is not None,
        ss_dim=k_total if row_ss is not None else 0, gain_rows=gain_rows, side_flags=side_flags)
    n_main = len(out_specs)
    outs = pl.pallas_call(
        kernel,
        grid=(n_i, n_j, nk),
        in_specs=in_specs + side_in,
        out_specs=out_specs + side_out,
        out_shape=out_shape + side_shapes,
        compiler_params=_params("parallel", "arbitrary" if n_gain else "parallel", "arbitrary"),
        name=name,
    )(*args, *side_args)
    outs = list(outs[:n_main]) + finalize(outs[n_main:])
    return outs[0] if len(outs) == 1 else tuple(outs)


def _pool_kernel(h_ref, w_ref, wpg_ref, scale_ref, *rest, tm, side_flags):
    rest = list(rest)
    side_jobs = _pop_side_inputs(rest, side_flags)
    o_ref = rest.pop(0)
    u_ref, ub_ref, p_ref, carry_ref = rest[len(side_flags):]
    _run_side_casts(side_jobs, rest[:len(side_flags)])
    i = pl.program_id(1)
    g = pl.program_id(2)
    blk = POOL_BLOCK
    half = tm // 2
    u_ref[:half, :] = jnp.dot(h_ref[:half, :], w_ref[...], preferred_element_type=F32)
    u_ref[half:, :] = jnp.dot(h_ref[half:, :], w_ref[...], preferred_element_type=F32)
    ub_ref[:blk - POOL_HALO, :] = jnp.zeros((blk - POOL_HALO, ub_ref.shape[1]), ub_ref.dtype)

    @pl.when(i == 0)
    def _():
        ub_ref[blk - POOL_HALO:blk, :] = jnp.zeros((POOL_HALO, ub_ref.shape[1]), ub_ref.dtype)

    @pl.when(i > 0)
    def _():
        ub_ref[blk - POOL_HALO:blk, :] = carry_ref[g]

    ub_ref[blk:, :] = u_ref[...].astype(ub_ref.dtype)
    carry_ref[g] = ub_ref[blk + tm - POOL_HALO:, :]

    window = jnp.left_shift(2, g)
    row = lax.broadcasted_iota(jnp.int32, (blk, 2 * blk), 0) + blk
    col = lax.broadcasted_iota(jnp.int32, (blk, 2 * blk), 1)
    band = jnp.where((col <= row) & (col > row - window), 1.0, 0.0).astype(BF16)
    pos = i * tm + lax.broadcasted_iota(jnp.int32, (tm, 1), 0)
    inv_cnt = 1.0 / jnp.minimum(pos + 1, window).astype(F32)
    for r in range(tm // blk):
        rows = slice(r * blk, (r + 1) * blk)
        s = jnp.dot(band, ub_ref[r * blk:(r + 2) * blk, :], preferred_element_type=F32)
        p_ref[rows, :] = (s * inv_cnt[rows] - u_ref[rows, :]).astype(p_ref.dtype)

    mixed = jnp.dot(p_ref[...], wpg_ref[...], preferred_element_type=F32) * scale_ref[...]
    o_ref[...] = mixed.astype(o_ref.dtype)


def _pool_branch(h, w_in, w_pg, scale, batch, seq, tm=1024, side_casts=()):
    n, d = h.shape
    groups, gw = w_pg.shape[0], w_pg.shape[1]
    assert POOL_WINDOWS == tuple(2 << g for g in range(groups)) and max(POOL_WINDOWS) <= POOL_HALO
    tm = min(tm, seq)
    n_i = seq // tm
    side_in, side_args, side_out, side_shapes, side_flags, finalize = _side_cast_plan(
        side_casts, batch * n_i * groups, lambda b, i, g: (b * n_i + i) * groups + g)
    outs = pl.pallas_call(
        functools.partial(_pool_kernel, tm=tm, side_flags=side_flags),
        grid=(batch, n_i, groups),
        in_specs=[pl.BlockSpec((tm, d), lambda b, i, g: (b * n_i + i, 0)),
                  pl.BlockSpec((d, gw), lambda b, i, g: (0, g)),
                  pl.BlockSpec((None, gw, gw), lambda b, i, g: (g, 0, 0)),
                  pl.BlockSpec((1, gw), lambda b, i, g: (0, g))] + side_in,
        out_specs=[pl.BlockSpec((tm, gw), lambda b, i, g: (b * n_i + i, g))] + side_out,
        out_shape=[jax.ShapeDtypeStruct((n, groups * gw), BF16)] + side_shapes,
        scratch_shapes=[pltpu.VMEM((tm, gw), F32), pltpu.VMEM((POOL_BLOCK + tm, gw), BF16),
                        pltpu.VMEM((tm, gw), BF16), pltpu.VMEM((groups, POOL_HALO, gw), BF16)],
        compiler_params=_params("parallel", "arbitrary", "arbitrary"),
        name="pool_branch",
    )(h, w_in, w_pg, scale.reshape(1, -1), *side_args)
    return outs[0] if not side_casts else (outs[0], *finalize(outs[1:]))


def _mem_attn_kernel(h_ref, wq_ref, k_ref, v_ref, *rest, scale, ss_dim, hd):
    rest = list(rest)
    ss_ref = rest.pop(0) if ss_dim else None
    (o_ref,) = rest
    q = jnp.dot(h_ref[...], wq_ref[...], preferred_element_type=F32)
    if ss_dim:
        q = q * _inv_rms(ss_ref, ss_dim)
    q = q.astype(BF16)
    for head in range(MEM_HEADS):
        cols = slice(head * hd, (head + 1) * hd)
        s = lax.dot_general(q[:, cols], k_ref[:, cols], (((1,), (1,)), ((), ())),
                            preferred_element_type=F32) * scale
        m = jnp.max(s, axis=-1, keepdims=True)
        p = jnp.exp(s - m)
        inv_den = 1.0 / jnp.sum(p, axis=-1, keepdims=True)
        o = jnp.dot(p.astype(BF16), v_ref[:, cols], preferred_element_type=F32) * inv_den
        o_ref[:, cols] = o.astype(o_ref.dtype)


def _mem_attn(h, w_in, q_col0, mkv, batch, seq, mem_len, tm=1024, row_ss=None):
    n, d = h.shape
    width = mkv.shape[1] // 2
    hd = width // MEM_HEADS
    tm = min(tm, seq)
    n_i = seq // tm
    assert q_col0 % width == 0
    in_specs = [pl.BlockSpec((tm, d), lambda b, i: (b * n_i + i, 0)),
                pl.BlockSpec((d, width), lambda b, i: (0, q_col0 // width)),
                pl.BlockSpec((mem_len, width), lambda b, i: (b, 0)),
                pl.BlockSpec((mem_len, width), lambda b, i: (b, 1))]
    args = [h, w_in, mkv, mkv]
    if row_ss is not None:
        in_specs.append(pl.BlockSpec((tm, LANES), lambda b, i: (b * n_i + i, 0)))
        args.append(row_ss)
    return pl.pallas_call(
        functools.partial(_mem_attn_kernel, scale=1.0 / math.sqrt(hd),
                          ss_dim=d if row_ss is not None else 0, hd=hd),
        grid=(batch, n_i),
        in_specs=in_specs,
        out_specs=pl.BlockSpec((tm, width), lambda b, i: (b * n_i + i, 0)),
        out_shape=jax.ShapeDtypeStruct((n, width), BF16),
        compiler_params=_params("parallel", "parallel"),
        name="mem_attn",
    )(*args)


def _t5_bucket(dist):
    max_exact = NUM_BUCKETS // 2
    d32 = jnp.maximum(dist, 1).astype(F32)
    large = max_exact + (jnp.log(d32 / max_exact) / math.log(MAX_DISTANCE / max_exact)
                         * (NUM_BUCKETS - max_exact)).astype(jnp.int32)
    large = jnp.minimum(large, NUM_BUCKETS - 1)
    return jnp.where(dist < max_exact, dist, large)


def _bucket_tables():
    qi = jnp.arange(ATT_BLOCK)[:, None]
    kj = jnp.arange(2 * ATT_BLOCK)[None, :]
    delta = qi + ATT_BLOCK - kj
    band = (delta >= 0) & (delta <= ATT_BLOCK)
    tabs = [jnp.where(band, _t5_bucket(jnp.maximum(delta, 0) * dil), -1) for _, dil in DIL_CONFIGS]
    return jnp.stack(tabs).astype(jnp.int32)


def _bias_table_kernel(rb_ref, bkt_ref, o_ref):
    gh = pl.program_id(0)
    bkt = bkt_ref[...]
    bias = jnp.full(bkt.shape, MASKED, F32)
    for b in range(NUM_BUCKETS):
        bias = jnp.where(bkt == b, rb_ref[b, gh], bias)
    o_ref[...] = bias


def _bias_tables(rel_bias, heads):
    n_gh = rel_bias.shape[1]
    tab = (ATT_BLOCK, 2 * ATT_BLOCK)
    return pl.pallas_call(
        _bias_table_kernel,
        grid=(n_gh,),
        in_specs=[pl.BlockSpec(memory_space=pltpu.SMEM),
                  pl.BlockSpec((None,) + tab, lambda gh: (gh // heads, 0, 0))],
        out_specs=pl.BlockSpec((None,) + tab, lambda gh: (gh, 0, 0)),
        out_shape=jax.ShapeDtypeStruct((n_gh,) + tab, F32),
        compiler_params=_params("parallel"),
        name="bias_tables",
    )(rel_bias.astype(F32), _bucket_tables())


def _dil_attn_kernel(b0, b1, b2, q0, q1, q2, k0, k1, k2, v0, v1, v2, o_ref,
                     og0, og1, og2, lg0, lg1, lg2, *, seq, scale):
    bias_refs = (b0, b1, b2)
    q_refs, k_refs, v_refs = (q0, q1, q2), (k0, k1, k2), (v0, v1, v2)
    o_nat, lse_nat = (og0, og1, og2), (lg0, lg1, lg2)
    blk = ATT_BLOCK

    def qk(q3, k3):
        return jnp.einsum("uqd,ukd->uqk", q3, k3, preferred_element_type=F32) * scale

    def pv(p3, v3):
        return jnp.einsum("uqk,ukd->uqd", p3.astype(BF16), v3, preferred_element_type=F32)

    for g, (_, dil) in enumerate(DIL_CONFIGS):
        sub_len = seq // dil
        n_blk = sub_len // blk
        units = dil * n_blk

        def blocks(ref):
            parts = []
            for r in range(dil):
                rows = pl.ds(r, sub_len, stride=dil) if dil > 1 else pl.ds(0, sub_len)
                parts.append(ref[rows, :].astype(BF16).reshape(n_blk, blk, HEAD_DIM))
            return parts[0] if dil == 1 else jnp.concatenate(parts, axis=0)

        q3, k3, v3 = blocks(q_refs[g]), blocks(k_refs[g]), blocks(v_refs[g])
        bias = bias_refs[g][...]
        s_cur = qk(q3, k3) + bias[None, :, blk:]
        m = jnp.max(s_cur, axis=-1, keepdims=True)
        if n_blk > 1:
            k_prev = jnp.concatenate([k3[:1], k3[:-1]], axis=0)
            v_prev = jnp.concatenate([v3[:1], v3[:-1]], axis=0)
            unit = lax.broadcasted_iota(jnp.int32, (units, 1, 1), 0)
            bias_prev = jnp.where(unit % n_blk == 0, MASKED, bias[None, :, :blk])
            s_prev = qk(q3, k_prev) + bias_prev
            m = jnp.maximum(m, jnp.max(s_prev, axis=-1, keepdims=True))
        p_cur = jnp.exp(s_cur - m)
        den = jnp.sum(p_cur, axis=-1, keepdims=True)
        acc = pv(p_cur, v3)
        if n_blk > 1:
            p_prev = jnp.exp(s_prev - m)
            den = den + jnp.sum(p_prev, axis=-1, keepdims=True)
            acc = acc + pv(p_prev, v_prev)
        o3 = acc * (1.0 / den)
        lse3 = m + jnp.log(den)
        for u in range(units):
            r, n = divmod(u, n_blk)
            start = r + dil * n * blk
            nat = pl.ds(start, blk, stride=dil) if dil > 1 else pl.ds(start, blk)
            o_nat[g][nat, :] = o3[u]
            lse_nat[g][nat, :] = jnp.broadcast_to(lse3[u], (blk, HEAD_DIM))

    l0, l1, l2 = lse_nat[0][...], lse_nat[1][...], lse_nat[2][...]
    top = jnp.maximum(jnp.maximum(l0, l1), l2)
    w0, w1, w2 = jnp.exp(l0 - top), jnp.exp(l1 - top), jnp.exp(l2 - top)
    inv = 1.0 / (w0 + w1 + w2)
    out = (o_nat[0][...] * w0 + o_nat[1][...] * w1 + o_nat[2][...] * w2) * inv
    o_ref[...] = out.astype(o_ref.dtype)


def _dil_attn(q, kv, rel_bias, batch, seq):
    n = q.shape[0]
    n_groups = len(DIL_CONFIGS)
    heads = rel_bias.shape[1] // n_groups
    gh = n_groups * heads
    bias = _bias_tables(rel_bias, heads)

    def col_spec(off):
        return pl.BlockSpec((seq, HEAD_DIM), lambda b, hh: (b, off + hh))

    in_specs = [pl.BlockSpec((None, ATT_BLOCK, 2 * ATT_BLOCK),
                             lambda b, hh, g=g: (g * heads + hh, 0, 0)) for g in range(n_groups)]
    in_specs += [col_spec(g * heads) for g in range(n_groups)]
    in_specs += [col_spec(g * heads) for g in range(n_groups)]
    in_specs += [col_spec(gh + g * heads) for g in range(n_groups)]
    return pl.pallas_call(
        functools.partial(_dil_attn_kernel, seq=seq, scale=1.0 / math.sqrt(HEAD_DIM)),
        grid=(batch, heads),
        in_specs=in_specs,
        out_specs=pl.BlockSpec((seq, HEAD_DIM), lambda b, hh: (b, hh)),
        out_shape=jax.ShapeDtypeStruct((n, heads * HEAD_DIM), BF16),
        scratch_shapes=[pltpu.VMEM((seq, HEAD_DIM), F32)] * (2 * n_groups),
        compiler_params=_params("parallel", "arbitrary"),
        name="dil_attn",
    )(bias, bias, bias, q, q, q, kv, kv, kv, kv, kv, kv)


def kernel(x, mem, a_norm, a_w_in, a_w_pg, a_scale, a_w_out, kv_norm, w_kv, b_norm, b_w_in,
           b_w_out, mem_norm, w_mem_kv, mlp_norm, mlp_w1, mlp_w2, rel_bias, final_norm):
    batch, seq, d = x.shape
    mem_len = mem.shape[1]
    n = batch * seq
    pool_width = a_scale.shape[-1]
    dil_q_width = w_kv.shape[1] // 2
    assert a_norm.shape[0] == 1 and b_norm.shape[0] == 1 and mlp_norm.shape[0] == 2

    x0 = x.reshape(n, d)
    (mem_h,) = _rmsnorm(mem.reshape(batch * mem_len, d), [mem_norm], BF16)
    groups, gw = a_w_pg.shape[1], a_w_pg.shape[2]
    h0, mkv, (a_in, a_pg) = _norm_and_mem_kv(
        x0, a_norm[0], mem_h, w_mem_kv,
        side_casts=[(0, a_w_in), (0, a_w_pg.reshape(1, groups * gw, gw))])

    mixed, a_out_w, b_out_w, w1 = _pool_branch(
        h0, a_in, a_pg.reshape(groups, gw, gw), a_scale[0], batch, seq,
        side_casts=[(0, a_w_out), (0, b_w_out), (0, mlp_w1)])
    mem_out = _mem_attn(h0, a_in, pool_width, mkv[0], batch, seq, mem_len)
    x1, xg, ss = _linear([mixed, mem_out], a_out_w, d, F32, res=x0, norm_gains=[mlp_norm[0]],
                         name="a_out")
    hid, w2, w1_next, kv_w, b_in = _linear(
        [xg], w1, mlp_w1.shape[2], BF16, act="relu2", row_ss=ss,
        side_casts=[(0, mlp_w2), (1, mlp_w1), (0, w_kv[None], kv_norm), (0, b_w_in, b_norm[0])],
        name="mlp_up")
    x2, xn, ss = _linear([hid], w2, d, F32, res=x1, norm_gains=[None], name="mlp_down")

    kv = _linear([xn], kv_w, w_kv.shape[1], F32, row_ss=ss, name="kv_proj")
    q = _linear([xn], b_in, dil_q_width, F32, row_ss=ss, name="q_proj")
    dil_out = _dil_attn(q, kv, rel_bias, batch, seq)
    mem_out = _mem_attn(xn, b_in, dil_q_width, mkv[1], batch, seq, mem_len, row_ss=ss)
    x3, xg, ss = _linear([dil_out, mem_out], b_out_w, d, F32, res=x2, norm_gains=[mlp_norm[1]],
                         name="b_out")
    hid, w2 = _linear([xg], w1_next, mlp_w1.shape[2], BF16, act="relu2", row_ss=ss,
                      side_casts=[(1, mlp_w2)], name="mlp_up")
    x4 = _linear([hid], w2, d, F32, res=x3, name="mlp_down")

    (out,) = _rmsnorm(x4, [final_norm], F32)
    return out.reshape(batch, seq, d)
```

```python
import functools
import math

import jax
import jax.numpy as jnp
from jax import lax
from jax.experimental import pallas as pl
from jax.experimental.pallas import tpu as pltpu

F32 = jnp.float32
BF16 = jnp.bfloat16

EPS = 1e-6
POOL_WINDOWS = (2, 4, 8, 16)
POOL_HALO = 16
POOL_BLOCK = 128
MEM_HEADS = 4
LANES = 128
HEAD_DIM = 128
DIL_CONFIGS = ((128, 1), (512, 4), (2048, 16))
ATT_BLOCK = 128
NUM_BUCKETS = 32
MAX_DISTANCE = 2048
MASKED = -1e30
V7X_VMEM_BYTES = 64 * 1024 * 1024
VMEM_LIMIT_BYTES = V7X_VMEM_BYTES - 1 * 1024 * 1024


def _params(*semantics):
    return pltpu.CompilerParams(dimension_semantics=semantics, vmem_limit_bytes=VMEM_LIMIT_BYTES)


def _rmsnorm_kernel(x_ref, g_ref, *o_refs):
    x = x_ref[...]
    y = x * lax.rsqrt(jnp.mean(x * x, axis=-1, keepdims=True) + EPS)
    for n, o_ref in enumerate(o_refs):
        o_ref[...] = (y * g_ref[n:n + 1, :]).astype(o_ref.dtype)


def _rmsnorm(x, gains, out_dtype, tm=512):
    m, d = x.shape
    n = len(gains)
    tm = math.gcd(tm, m)
    g = jnp.stack(gains).astype(F32)
    outs = pl.pallas_call(
        _rmsnorm_kernel,
        grid=(m // tm,),
        in_specs=[pl.BlockSpec((tm, d), lambda i: (i, 0)),
                  pl.BlockSpec((n, d), lambda i: (0, 0))],
        out_specs=[pl.BlockSpec((tm, d), lambda i: (i, 0))] * n,
        out_shape=[jax.ShapeDtypeStruct((m, d), out_dtype)] * n,
        compiler_params=_params("parallel"),
        name="rmsnorm",
    )(x, g)
    return list(outs)


def _norm_and_mem_kv_kernel(x_ref, g_ref, mem_ref, w_ref, *rest, side_flags):
    rest = list(rest)
    side_jobs = _pop_side_inputs(rest, side_flags)
    h_ref, kv_ref = rest[:2]
    _run_side_casts(side_jobs, rest[2:])
    x = x_ref[...]
    y = x * lax.rsqrt(jnp.mean(x * x, axis=-1, keepdims=True) + EPS)
    h_ref[...] = (y * g_ref[...]).astype(h_ref.dtype)
    w = w_ref[...].astype(BF16)
    kv_ref[...] = jnp.dot(mem_ref[...], w, preferred_element_type=F32).astype(kv_ref.dtype)


def _norm_and_mem_kv(x, gain, mem_h, w_mem_kv, tn=LANES, side_casts=()):
    n, d = x.shape
    layers, _, n_out = w_mem_kv.shape
    m = mem_h.shape[0]
    tiles = n_out // tn
    steps = layers * tiles
    slab = n // steps
    assert tiles * tn == n_out and slab * steps == n and slab % 16 == 0
    side_in, side_args, side_out, side_shapes, side_flags, finalize = _side_cast_plan(
        side_casts, steps, lambda s: s)
    h, kv, *sides = pl.pallas_call(
        functools.partial(_norm_and_mem_kv_kernel, side_flags=side_flags),
        grid=(steps,),
        in_specs=[pl.BlockSpec((slab, d), lambda s: (s, 0)),
                  pl.BlockSpec((1, d), lambda s: (0, 0)),
                  pl.BlockSpec((m, d), lambda s: (0, 0), pipeline_mode=pl.Buffered(1)),
                  pl.BlockSpec((d, tn), lambda s: (s // tiles, s % tiles))] + side_in,
        out_specs=[pl.BlockSpec((slab, d), lambda s: (s, 0)),
                   pl.BlockSpec((None, m, tn), lambda s: (s // tiles, 0, s % tiles))] + side_out,
        out_shape=[jax.ShapeDtypeStruct((n, d), BF16),
                   jax.ShapeDtypeStruct((layers, m, n_out), BF16)] + side_shapes,
        compiler_params=_params("parallel"),
        name="norm_and_mem_kv",
    )(x, gain.reshape(1, d).astype(F32), mem_h, w_mem_kv.reshape(layers * d, n_out), *side_args)
    return h, [kv[l] for l in range(layers)], finalize(sides)


def _side_cast_plan(side_casts, steps, step_index):
    in_specs, args, out_specs, out_shapes, shapes, flags = [], [], [], [], [], []
    for layer, stack, *gain in side_casts:
        _, rows, cols = stack.shape
        fold = 1
        while (rows * fold) % (steps * 16):
            fold *= 2
        assert fold == 1 or (cols % (fold * LANES) == 0 and not gain)
        r, c = rows * fold, cols // fold
        slab = r // steps
        in_specs.append(pl.BlockSpec(
            (slab, c), lambda *g, layer=layer: (layer * steps + step_index(*g), 0)))
        args.append(stack.reshape(-1, c))
        if gain:
            in_specs.append(pl.BlockSpec((slab, LANES), lambda *g: (step_index(*g), 0)))
            args.append(jnp.broadcast_to(gain[0].astype(F32)[:, None], (rows, LANES)))
        out_specs.append(pl.BlockSpec((slab, c), lambda *g: (step_index(*g), 0)))
        out_shapes.append(jax.ShapeDtypeStruct((r, c), BF16))
        shapes.append((rows, cols))
        flags.append(bool(gain))

    def finalize(outs):
        return [o.reshape(s) for o, s in zip(outs, shapes)]

    return in_specs, args, out_specs, out_shapes, tuple(flags), finalize


def _pop_side_inputs(refs, side_flags):
    jobs = []
    for has_gain in side_flags:
        src_ref = refs.pop(0)
        jobs.append((src_ref, refs.pop(0) if has_gain else None))
    return jobs


def _run_side_casts(jobs, dst_refs):
    for (src_ref, gain_ref), dst_ref in zip(jobs, dst_refs):
        if gain_ref is None:
            dst_ref[...] = src_ref[...].astype(dst_ref.dtype)
            continue
        g = gain_ref[...]
        for c in range(0, src_ref.shape[1], LANES):
            dst_ref[:, c:c + LANES] = (src_ref[:, c:c + LANES] * g).astype(dst_ref.dtype)


def _inv_rms(ss_ref, dim):
    return lax.rsqrt(ss_ref[:, 0:1] * (1.0 / dim) + EPS)


def _linear_kernel(*refs, k_sizes, nk, act, has_res, ss_dim, gain_rows, side_flags):
    refs = list(refs)
    lhs_refs = [refs.pop(0) for _ in k_sizes]
    w_ref = refs.pop(0)
    res_ref = refs.pop(0) if has_res else None
    in_ss_ref = refs.pop(0) if ss_dim else None
    gain_ref = refs.pop(0) if any(r is not None for r in gain_rows) else None
    side_jobs = _pop_side_inputs(refs, side_flags)
    o_ref = refs.pop(0)
    scaled_refs = [refs.pop(0) for _ in gain_rows]
    out_ss_ref = refs.pop(0) if gain_rows else None
    n_gain = len(gain_rows)
    _run_side_casts(side_jobs, refs)
    j = pl.program_id(1)

    def product():
        y, off = None, 0
        for lhs_ref, ks in zip(lhs_refs, k_sizes):
            t = jnp.dot(lhs_ref[...], w_ref[off:off + ks, :], preferred_element_type=F32)
            y = t if y is None else y + t
            off += ks
        return y

    def emit_norm_inputs(y):
        for row, s_ref in zip(gain_rows, scaled_refs):
            scaled = y if row is None else y * gain_ref[row:row + 1, :]
            s_ref[...] = scaled.astype(s_ref.dtype)
        part = jnp.broadcast_to(jnp.sum(y * y, axis=-1, keepdims=True), out_ss_ref.shape)

        @pl.when(j == 0)
        def _():
            out_ss_ref[...] = part

        @pl.when(j > 0)
        def _():
            out_ss_ref[...] += part

    if nk == 1:
        y = product()
        if ss_dim:
            y = y * _inv_rms(in_ss_ref, ss_dim)
        if act == "relu2":
            y = jnp.maximum(y, 0.0)
            y = y * y
        if has_res:
            y = res_ref[...] + y
        o_ref[...] = y.astype(o_ref.dtype)
        if n_gain:
            emit_norm_inputs(y)
        return

    k = pl.program_id(2)

    @pl.when(k == 0)
    def _():
        o_ref[...] = res_ref[...] if has_res else jnp.zeros_like(o_ref)

    if not n_gain:
        o_ref[...] += product()
        return

    @pl.when(k < nk - 1)
    def _():
        o_ref[...] += product()

    @pl.when(k == nk - 1)
    def _():
        y = o_ref[...] + product()
        o_ref[...] = y
        emit_norm_inputs(y)


def _linear(lhs_list, w, n_out, out_dtype, *, res=None, act=None, row_ss=None, norm_gains=(),
            tm=1024, tn=1024, tk=4096, side_casts=(), name="linear"):
    m = lhs_list[0].shape[0]
    k_sizes = [a.shape[1] for a in lhs_list]
    k_total = sum(k_sizes)
    assert w.shape[0] == k_total
    tm, tn = math.gcd(tm, m), math.gcd(tn, n_out)
    if len(lhs_list) > 1:
        assert k_total <= tk
        tk = k_total
    else:
        tk = math.gcd(tk, k_total)
        k_sizes = [tk]
    nk = k_total // tk
    assert nk == 1 or (act is None and out_dtype == F32 and row_ss is None)
    n_i, n_j = m // tm, n_out // tn
    n_gain = len(norm_gains)

    def tile(i, j, k):
        return (i, j)

    def row_block(i, j, k):
        return (i, 0)

    in_specs = [pl.BlockSpec((tm, ks), lambda i, j, k: (i, k)) for ks in k_sizes]
    in_specs.append(pl.BlockSpec((tk, tn), lambda i, j, k: (k, j)))
    args = list(lhs_list) + [w]
    if res is not None:
        in_specs.append(pl.BlockSpec((tm, tn), tile))
        args.append(res)
    if row_ss is not None:
        in_specs.append(pl.BlockSpec((tm, LANES), row_block))
        args.append(row_ss)
    gains = [g for g in norm_gains if g is not None]
    gain_rows = tuple(None if g is None else sum(h is not None for h in norm_gains[:n])
                      for n, g in enumerate(norm_gains))
    if gains:
        in_specs.append(pl.BlockSpec((len(gains), tn), lambda i, j, k: (0, j)))
        args.append(jnp.stack(gains).astype(F32))
    out_specs = [pl.BlockSpec((tm, tn), tile)] * (1 + n_gain)
    out_shape = [jax.ShapeDtypeStruct((m, n_out), out_dtype)]
    out_shape += [jax.ShapeDtypeStruct((m, n_out), BF16)] * n_gain
    if n_gain:
        out_specs.append(pl.BlockSpec((tm, LANES), row_block))
        out_shape.append(jax.ShapeDtypeStruct((m, LANES), F32))
    side_in, side_args, side_out, side_shapes, side_flags, finalize = _side_cast_plan(
        side_casts, n_i * n_j * nk, lambda i, j, k: (i * n_j + j) * nk + k)
    kernel = functools.partial(
        _linear_kernel, k_sizes=tuple(k_sizes), nk=nk, act=act, has_res=res is not None,
        ss_dim=k_total if row_ss is not None else 0, gain_rows=gain_rows, side_flags=side_flags)
    n_main = len(out_specs)
    outs = pl.pallas_call(
        kernel,
        grid=(n_i, n_j, nk),
        in_specs=in_specs + side_in,
        out_specs=out_specs + side_out,
        out_shape=out_shape + side_shapes,
        compiler_params=_params("parallel", "arbitrary" if n_gain else "parallel", "arbitrary"),
        name=name,
    )(*args, *side_args)
    outs = list(outs[:n_main]) + finalize(outs[n_main:])
    return outs[0] if len(outs) == 1 else tuple(outs)


def _pool_kernel(h_ref, w_ref, wpg_ref, scale_ref, *rest, tm, side_flags):
    rest = list(rest)
    side_jobs = _pop_side_inputs(rest, side_flags)
    o_ref = rest.pop(0)
    u_ref, ub_ref, p_ref, carry_ref = rest[len(side_flags):]
    _run_side_casts(side_jobs, rest[:len(side_flags)])
    i = pl.program_id(1)
    g = pl.program_id(2)
    blk = POOL_BLOCK
    half = tm // 2
    u_ref[:half, :] = jnp.dot(h_ref[:half, :], w_ref[...], preferred_element_type=F32)
    u_ref[half:, :] = jnp.dot(h_ref[half:, :], w_ref[...], preferred_element_type=F32)
    ub_ref[:blk - POOL_HALO, :] = jnp.zeros((blk - POOL_HALO, ub_ref.shape[1]), ub_ref.dtype)

    @pl.when(i == 0)
    def _():
        ub_ref[blk - POOL_HALO:blk, :] = jnp.zeros((POOL_HALO, ub_ref.shape[1]), ub_ref.dtype)

    @pl.when(i > 0)
    def _():
        ub_ref[blk - POOL_HALO:blk, :] = carry_ref[g]

    ub_ref[blk:, :] = u_ref[...].astype(ub_ref.dtype)
    carry_ref[g] = ub_ref[blk + tm - POOL_HALO:, :]

    window = jnp.left_shift(2, g)
    row = lax.broadcasted_iota(jnp.int32, (blk, 2 * blk), 0) + blk
    col = lax.broadcasted_iota(jnp.int32, (blk, 2 * blk), 1)
    band = jnp.where((col <= row) & (col > row - window), 1.0, 0.0).astype(BF16)
    pos = i * tm + lax.broadcasted_iota(jnp.int32, (tm, 1), 0)
    inv_cnt = 1.0 / jnp.minimum(pos + 1, window).astype(F32)
    for r in range(tm // blk):
        rows = slice(r * blk, (r + 1) * blk)
        s = jnp.dot(band, ub_ref[r * blk:(r + 2) * blk, :], preferred_element_type=F32)
        p_ref[rows, :] = (s * inv_cnt[rows] - u_ref[rows, :]).astype(p_ref.dtype)

    mixed = jnp.dot(p_ref[...], wpg_ref[...], preferred_element_type=F32) * scale_ref[...]
    o_ref[...] = mixed.astype(o_ref.dtype)


def _pool_branch(h, w_in, w_pg, scale, batch, seq, tm=1024, side_casts=()):
    n, d = h.shape
    groups, gw = w_pg.shape[0], w_pg.shape[1]
    assert POOL_WINDOWS == tuple(2 << g for g in range(groups)) and max(POOL_WINDOWS) <= POOL_HALO
    tm = min(tm, seq)
    n_i = seq // tm
    side_in, side_args, side_out, side_shapes, side_flags, finalize = _side_cast_plan(
        side_casts, batch * n_i * groups, lambda b, i, g: (b * n_i + i) * groups + g)
    outs = pl.pallas_call(
        functools.partial(_pool_kernel, tm=tm, side_flags=side_flags),
        grid=(batch, n_i, groups),
        in_specs=[pl.BlockSpec((tm, d), lambda b, i, g: (b * n_i + i, 0)),
                  pl.BlockSpec((d, gw), lambda b, i, g: (0, g)),
                  pl.BlockSpec((None, gw, gw), lambda b, i, g: (g, 0, 0)),
                  pl.BlockSpec((1, gw), lambda b, i, g: (0, g))] + side_in,
        out_specs=[pl.BlockSpec((tm, gw), lambda b, i, g: (b * n_i + i, g))] + side_out,
        out_shape=[jax.ShapeDtypeStruct((n, groups * gw), BF16)] + side_shapes,
        scratch_shapes=[pltpu.VMEM((tm, gw), F32), pltpu.VMEM((POOL_BLOCK + tm, gw), BF16),
                        pltpu.VMEM((tm, gw), BF16), pltpu.VMEM((groups, POOL_HALO, gw), BF16)],
        compiler_params=_params("parallel", "arbitrary", "arbitrary"),
        name="pool_branch",
    )(h, w_in, w_pg, scale.reshape(1, -1), *side_args)
    return outs[0] if not side_casts else (outs[0], *finalize(outs[1:]))


def _mem_attn_kernel(h_ref, wq_ref, k_ref, v_ref, *rest, scale, ss_dim, hd):
    rest = list(rest)
    ss_ref = rest.pop(0) if ss_dim else None
    (o_ref,) = rest
    q = jnp.dot(h_ref[...], wq_ref[...], preferred_element_type=F32)
    if ss_dim:
        q = q * _inv_rms(ss_ref, ss_dim)
    q = q.astype(BF16)
    for head in range(MEM_HEADS):
        cols = slice(head * hd, (head + 1) * hd)
        s = lax.dot_general(q[:, cols], k_ref[:, cols], (((1,), (1,)), ((), ())),
                            preferred_element_type=F32) * scale
        m = jnp.max(s, axis=-1, keepdims=True)
        p = jnp.exp(s - m)
        inv_den = 1.0 / jnp.sum(p, axis=-1, keepdims=True)
        o = jnp.dot(p.astype(BF16), v_ref[:, cols], preferred_element_type=F32) * inv_den
        o_ref[:, cols] = o.astype(o_ref.dtype)


def _mem_attn(h, w_in, q_col0, mkv, batch, seq, mem_len, tm=1024, row_ss=None):
    n, d = h.shape
    width = mkv.shape[1] // 2
    hd = width // MEM_HEADS
    tm = min(tm, seq)
    n_i = seq // tm
    assert q_col0 % width == 0
    in_specs = [pl.BlockSpec((tm, d), lambda b, i: (b * n_i + i, 0)),
                pl.BlockSpec((d, width), lambda b, i: (0, q_col0 // width)),
                pl.BlockSpec((mem_len, width), lambda b, i: (b, 0)),
                pl.BlockSpec((mem_len, width), lambda b, i: (b, 1))]
    args = [h, w_in, mkv, mkv]
    if row_ss is not None:
        in_specs.append(pl.BlockSpec((tm, LANES), lambda b, i: (b * n_i + i, 0)))
        args.append(row_ss)
    return pl.pallas_call(
        functools.partial(_mem_attn_kernel, scale=1.0 / math.sqrt(hd),
                          ss_dim=d if row_ss is not None else 0, hd=hd),
        grid=(batch, n_i),
        in_specs=in_specs,
        out_specs=pl.BlockSpec((tm, width), lambda b, i: (b * n_i + i, 0)),
        out_shape=jax.ShapeDtypeStruct((n, width), BF16),
        compiler_params=_params("parallel", "parallel"),
        name="mem_attn",
    )(*args)


def _t5_bucket(dist):
    max_exact = NUM_BUCKETS // 2
    d32 = jnp.maximum(dist, 1).astype(F32)
    large = max_exact + (jnp.log(d32 / max_exact) / math.log(MAX_DISTANCE / max_exact)
                         * (NUM_BUCKETS - max_exact)).astype(jnp.int32)
    large = jnp.minimum(large, NUM_BUCKETS - 1)
    return jnp.where(dist < max_exact, dist, large)


def _bucket_tables():
    qi = jnp.arange(ATT_BLOCK)[:, None]
    kj = jnp.arange(2 * ATT_BLOCK)[None, :]
    delta = qi + ATT_BLOCK - kj
    band = (delta >= 0) & (delta <= ATT_BLOCK)
    tabs = [jnp.where(band, _t5_bucket(jnp.maximum(delta, 0) * dil), -1) for _, dil in DIL_CONFIGS]
    return jnp.stack(tabs).astype(jnp.int32)


def _bias_table_kernel(rb_ref, bkt_ref, o_ref):
    gh = pl.program_id(0)
    bkt = bkt_ref[...]
    bias = jnp.full(bkt.shape, MASKED, F32)
    for b in range(NUM_BUCKETS):
        bias = jnp.where(bkt == b, rb_ref[b, gh], bias)
    o_ref[...] = bias


def _bias_tables(rel_bias, heads):
    n_gh = rel_bias.shape[1]
    tab = (ATT_BLOCK, 2 * ATT_BLOCK)
    return pl.pallas_call(
        _bias_table_kernel,
        grid=(n_gh,),
        in_specs=[pl.BlockSpec(memory_space=pltpu.SMEM),
                  pl.BlockSpec((None,) + tab, lambda gh: (gh // heads, 0, 0))],
        out_specs=pl.BlockSpec((None,) + tab, lambda gh: (gh, 0, 0)),
        out_shape=jax.ShapeDtypeStruct((n_gh,) + tab, F32),
        compiler_params=_params("parallel"),
        name="bias_tables",
    )(rel_bias.astype(F32), _bucket_tables())


def _dil_attn_kernel(b0, b1, b2, q0, q1, q2, k0, k1, k2, v0, v1, v2, o_ref,
                     og0, og1, og2, lg0, lg1, lg2, *, seq, scale):
    bias_refs = (b0, b1, b2)
    q_refs, k_refs, v_refs = (q0, q1, q2), (k0, k1, k2), (v0, v1, v2)
    o_nat, lse_nat = (og0, og1, og2), (lg0, lg1, lg2)
    blk = ATT_BLOCK

    def qk(q3, k3):
        return jnp.einsum("uqd,ukd->uqk", q3, k3, preferred_element_type=F32) * scale

    def pv(p3, v3):
        return jnp.einsum("uqk,ukd->uqd", p3.astype(BF16), v3, preferred_element_type=F32)

    for g, (_, dil) in enumerate(DIL_CONFIGS):
        sub_len = seq // dil
        n_blk = sub_len // blk
        units = dil * n_blk

        def blocks(ref):
            parts = []
            for r in range(dil):
                rows = pl.ds(r, sub_len, stride=dil) if dil > 1 else pl.ds(0, sub_len)
                parts.append(ref[rows, :].astype(BF16).reshape(n_blk, blk, HEAD_DIM))
            return parts[0] if dil == 1 else jnp.concatenate(parts, axis=0)

        q3, k3, v3 = blocks(q_refs[g]), blocks(k_refs[g]), blocks(v_refs[g])
        bias = bias_refs[g][...]
        s_cur = qk(q3, k3) + bias[None, :, blk:]
        m = jnp.max(s_cur, axis=-1, keepdims=True)
        if n_blk > 1:
            k_prev = jnp.concatenate([k3[:1], k3[:-1]], axis=0)
            v_prev = jnp.concatenate([v3[:1], v3[:-1]], axis=0)
            unit = lax.broadcasted_iota(jnp.int32, (units, 1, 1), 0)
            bias_prev = jnp.where(unit % n_blk == 0, MASKED, bias[None, :, :blk])
            s_prev = qk(q3, k_prev) + bias_prev
            m = jnp.maximum(m, jnp.max(s_prev, axis=-1, keepdims=True))
        ones = jnp.ones_like(v3)
        acc = pv(jnp.exp(s_cur - m), jnp.concatenate([v3, ones], axis=-1))
        if n_blk > 1:
            acc = acc + pv(jnp.exp(s_prev - m), jnp.concatenate([v_prev, ones], axis=-1))
        den = acc[:, :, HEAD_DIM:]
        o3 = acc[:, :, :HEAD_DIM] * (1.0 / den)
        lse3 = m + jnp.log(den)
        for u in range(units):
            r, n = divmod(u, n_blk)
            start = r + dil * n * blk
            nat = pl.ds(start, blk, stride=dil) if dil > 1 else pl.ds(start, blk)
            o_nat[g][nat, :] = o3[u]
            lse_nat[g][nat, :] = jnp.broadcast_to(lse3[u], (blk, HEAD_DIM))

    l0, l1, l2 = lse_nat[0][...], lse_nat[1][...], lse_nat[2][...]
    top = jnp.maximum(jnp.maximum(l0, l1), l2)
    w0, w1, w2 = jnp.exp(l0 - top), jnp.exp(l1 - top), jnp.exp(l2 - top)
    inv = 1.0 / (w0 + w1 + w2)
    out = (o_nat[0][...] * w0 + o_nat[1][...] * w1 + o_nat[2][...] * w2) * inv
    o_ref[...] = out.astype(o_ref.dtype)


def _dil_attn(q, kv, rel_bias, batch, seq):
    n = q.shape[0]
    n_groups = len(DIL_CONFIGS)
    heads = rel_bias.shape[1] // n_groups
    gh = n_groups * heads
    bias = _bias_tables(rel_bias, heads)

    def col_spec(off):
        return pl.BlockSpec((seq, HEAD_DIM), lambda b, hh: (b, off + hh))

    in_specs = [pl.BlockSpec((None, ATT_BLOCK, 2 * ATT_BLOCK),
                             lambda b, hh, g=g: (g * heads + hh, 0, 0)) for g in range(n_groups)]
    in_specs += [col_spec(g * heads) for g in range(n_groups)]
    in_specs += [col_spec(g * heads) for g in range(n_groups)]
    in_specs += [col_spec(gh + g * heads) for g in range(n_groups)]
    return pl.pallas_call(
        functools.partial(_dil_attn_kernel, seq=seq, scale=1.0 / math.sqrt(HEAD_DIM)),
        grid=(batch, heads),
        in_specs=in_specs,
        out_specs=pl.BlockSpec((seq, HEAD_DIM), lambda b, hh: (b, hh)),
        out_shape=jax.ShapeDtypeStruct((n, heads * HEAD_DIM), BF16),
        scratch_shapes=[pltpu.VMEM((seq, HEAD_DIM), F32)] * (2 * n_groups),
        compiler_params=_params("parallel", "arbitrary"),
        name="dil_attn",
    )(bias, bias, bias, q, q, q, kv, kv, kv, kv, kv, kv)


def kernel(x, mem, a_norm, a_w_in, a_w_pg, a_scale, a_w_out, kv_norm, w_kv, b_norm, b_w_in,
           b_w_out, mem_norm, w_mem_kv, mlp_norm, mlp_w1, mlp_w2, rel_bias, final_norm):
    batch, seq, d = x.shape
    mem_len = mem.shape[1]
    n = batch * seq
    pool_width = a_scale.shape[-1]
    dil_q_width = w_kv.shape[1] // 2
    assert a_norm.shape[0] == 1 and b_norm.shape[0] == 1 and mlp_norm.shape[0] == 2

    x0 = x.reshape(n, d)
    (mem_h,) = _rmsnorm(mem.reshape(batch * mem_len, d), [mem_norm], BF16)
    groups, gw = a_w_pg.shape[1], a_w_pg.shape[2]
    h0, mkv, (a_in, a_pg) = _norm_and_mem_kv(
        x0, a_norm[0], mem_h, w_mem_kv,
        side_casts=[(0, a_w_in), (0, a_w_pg.reshape(1, groups * gw, gw))])

    mixed, a_out_w, b_out_w, w1 = _pool_branch(
        h0, a_in, a_pg.reshape(groups, gw, gw), a_scale[0], batch, seq,
        side_casts=[(0, a_w_out), (0, b_w_out), (0, mlp_w1)])
    mem_out = _mem_attn(h0, a_in, pool_width, mkv[0], batch, seq, mem_len)
    x1, xg, ss = _linear([mixed, mem_out], a_out_w, d, F32, res=x0, norm_gains=[mlp_norm[0]],
                         name="a_out")
    hid, w2, w1_next, kv_w, b_in = _linear(
        [xg], w1, mlp_w1.shape[2], BF16, act="relu2", row_ss=ss,
        side_casts=[(0, mlp_w2), (1, mlp_w1), (0, w_kv[None], kv_norm), (0, b_w_in, b_norm[0])],
        name="mlp_up")
    x2, xn, ss = _linear([hid], w2, d, F32, res=x1, norm_gains=[None], name="mlp_down")

    kv = _linear([xn], kv_w, w_kv.shape[1], F32, row_ss=ss, name="kv_proj")
    q = _linear([xn], b_in, dil_q_width, F32, row_ss=ss, name="q_proj")
    dil_out = _dil_attn(q, kv, rel_bias, batch, seq)
    mem_out = _mem_attn(xn, b_in, dil_q_width, mkv[1], batch, seq, mem_len, row_ss=ss)
    x3, xg, ss = _linear([dil_out, mem_out], b_out_w, d, F32, res=x2, norm_gains=[mlp_norm[1]],
                         name="b_out")
    hid, w2 = _linear([xg], w1_next, mlp_w1.shape[2], BF16, act="relu2", row_ss=ss,
                      side_casts=[(1, mlp_w2)], name="mlp_up")
    x4 = _linear([hid], w2, d, F32, res=x3, name="mlp_down")

    (out,) = _rmsnorm(x4, [final_norm], F32)
    return out.reshape(batch, seq, d)
```

```python
import functools
import math

import jax
import jax.numpy as jnp
from jax import lax
from jax.experimental import pallas as pl
from jax.experimental.pallas import tpu as pltpu

F32 = jnp.float32
BF16 = jnp.bfloat16

EPS = 1e-6
POOL_WINDOWS = (2, 4, 8, 16)
POOL_HALO = 16
POOL_BLOCK = 128
MEM_HEADS = 4
LANES = 128
HEAD_DIM = 128
DIL_CONFIGS = ((128, 1), (512, 4), (2048, 16))
ATT_BLOCK = 128
NUM_BUCKETS = 32
MAX_DISTANCE = 2048
MASKED = -1e30
V7X_VMEM_BYTES = 64 * 1024 * 1024
VMEM_LIMIT_BYTES = V7X_VMEM_BYTES - 1 * 1024 * 1024


def _params(*semantics):
    return pltpu.CompilerParams(dimension_semantics=semantics, vmem_limit_bytes=VMEM_LIMIT_BYTES)


def _rmsnorm_kernel(x_ref, g_ref, *o_refs):
    x = x_ref[...]
    y = x * lax.rsqrt(jnp.mean(x * x, axis=-1, keepdims=True) + EPS)
    for n, o_ref in enumerate(o_refs):
        o_ref[...] = (y * g_ref[n:n + 1, :]).astype(o_ref.dtype)


def _rmsnorm(x, gains, out_dtype, tm=512):
    m, d = x.shape
    n = len(gains)
    tm = math.gcd(tm, m)
    g = jnp.stack(gains).astype(F32)
    outs = pl.pallas_call(
        _rmsnorm_kernel,
        grid=(m // tm,),
        in_specs=[pl.BlockSpec((tm, d), lambda i: (i, 0)),
                  pl.BlockSpec((n, d), lambda i: (0, 0))],
        out_specs=[pl.BlockSpec((tm, d), lambda i: (i, 0))] * n,
        out_shape=[jax.ShapeDtypeStruct((m, d), out_dtype)] * n,
        compiler_params=_params("parallel"),
        name="rmsnorm",
    )(x, g)
    return list(outs)


def _norm_and_mem_kv_kernel(x_ref, g_ref, mem_ref, w_ref, *rest, side_flags):
    rest = list(rest)
    side_jobs = _pop_side_inputs(rest, side_flags)
    h_ref, kv_ref = rest[:2]
    _run_side_casts(side_jobs, rest[2:])
    x = x_ref[...]
    y = x * lax.rsqrt(jnp.mean(x * x, axis=-1, keepdims=True) + EPS)
    h_ref[...] = (y * g_ref[...]).astype(h_ref.dtype)
    w = w_ref[...].astype(BF16)
    kv_ref[...] = jnp.dot(mem_ref[...], w, preferred_element_type=F32).astype(kv_ref.dtype)


def _norm_and_mem_kv(x, gain, mem_h, w_mem_kv, tn=LANES, side_casts=()):
    n, d = x.shape
    layers, _, n_out = w_mem_kv.shape
    m = mem_h.shape[0]
    tiles = n_out // tn
    steps = layers * tiles
    slab = n // steps
    assert tiles * tn == n_out and slab * steps == n and slab % 16 == 0
    side_in, side_args, side_out, side_shapes, side_flags, finalize = _side_cast_plan(
        side_casts, steps, lambda s: s)
    h, kv, *sides = pl.pallas_call(
        functools.partial(_norm_and_mem_kv_kernel, side_flags=side_flags),
        grid=(steps,),
        in_specs=[pl.BlockSpec((slab, d), lambda s: (s, 0)),
                  pl.BlockSpec((1, d), lambda s: (0, 0)),
                  pl.BlockSpec((m, d), lambda s: (0, 0), pipeline_mode=pl.Buffered(1)),
                  pl.BlockSpec((d, tn), lambda s: (s // tiles, s % tiles))] + side_in,
        out_specs=[pl.BlockSpec((slab, d), lambda s: (s, 0)),
                   pl.BlockSpec((None, m, tn), lambda s: (s // tiles, 0, s % tiles))] + side_out,
        out_shape=[jax.ShapeDtypeStruct((n, d), BF16),
                   jax.ShapeDtypeStruct((layers, m, n_out), BF16)] + side_shapes,
        compiler_params=_params("parallel"),
        name="norm_and_mem_kv",
    )(x, gain.reshape(1, d).astype(F32), mem_h, w_mem_kv.reshape(layers * d, n_out), *side_args)
    return h, [kv[l] for l in range(layers)], finalize(sides)


def _side_cast_plan(side_casts, steps, step_index):
    in_specs, args, out_specs, out_shapes, shapes, flags = [], [], [], [], [], []
    for layer, stack, *gain in side_casts:
        _, rows, cols = stack.shape
        fold = 1
        while (rows * fold) % (steps * 16):
            fold *= 2
        assert fold == 1 or (cols % (fold * LANES) == 0 and not gain)
        r, c = rows * fold, cols // fold
        slab = r // steps
        in_specs.append(pl.BlockSpec(
            (slab, c), lambda *g, layer=layer: (layer * steps + step_index(*g), 0)))
        args.append(stack.reshape(-1, c))
        if gain:
            in_specs.append(pl.BlockSpec((slab, LANES), lambda *g: (step_index(*g), 0)))
            args.append(jnp.broadcast_to(gain[0].astype(F32)[:, None], (rows, LANES)))
        out_specs.append(pl.BlockSpec((slab, c), lambda *g: (step_index(*g), 0)))
        out_shapes.append(jax.ShapeDtypeStruct((r, c), BF16))
        shapes.append((rows, cols))
        flags.append(bool(gain))

    def finalize(outs):
        return [o.reshape(s) for o, s in zip(outs, shapes)]

    return in_specs, args, out_specs, out_shapes, tuple(flags), finalize


def _pop_side_inputs(refs, side_flags):
    jobs = []
    for has_gain in side_flags:
        src_ref = refs.pop(0)
        jobs.append((src_ref, refs.pop(0) if has_gain else None))
    return jobs


def _run_side_casts(jobs, dst_refs):
    for (src_ref, gain_ref), dst_ref in zip(jobs, dst_refs):
        if gain_ref is None:
            dst_ref[...] = src_ref[...].astype(dst_ref.dtype)
            continue
        g = gain_ref[...]
        for c in range(0, src_ref.shape[1], LANES):
            dst_ref[:, c:c + LANES] = (src_ref[:, c:c + LANES] * g).astype(dst_ref.dtype)


def _inv_rms(ss_ref, dim):
    return lax.rsqrt(ss_ref[:, 0:1] * (1.0 / dim) + EPS)


def _linear_kernel(*refs, k_sizes, nk, act, has_res, ss_dim, gain_rows, side_flags, resident_tn):
    refs = list(refs)
    lhs_refs = [refs.pop(0) for _ in k_sizes]
    w_ref = refs.pop(0)
    res_ref = refs.pop(0) if has_res else None
    in_ss_ref = refs.pop(0) if ss_dim else None
    gain_ref = refs.pop(0) if any(r is not None for r in gain_rows) else None
    side_jobs = _pop_side_inputs(refs, side_flags)
    o_ref = refs.pop(0)
    scaled_refs = [refs.pop(0) for _ in gain_rows]
    out_ss_ref = refs.pop(0) if gain_rows else None
    n_gain = len(gain_rows)
    _run_side_casts(side_jobs, refs)
    j = pl.program_id(1)
    w_cols = pl.ds(pl.multiple_of(j * resident_tn, resident_tn), resident_tn) if resident_tn \
        else slice(None)

    def product():
        y, off = None, 0
        for lhs_ref, ks in zip(lhs_refs, k_sizes):
            t = jnp.dot(lhs_ref[...], w_ref[off:off + ks, w_cols], preferred_element_type=F32)
            y = t if y is None else y + t
            off += ks
        return y

    def emit_norm_inputs(y):
        for row, s_ref in zip(gain_rows, scaled_refs):
            scaled = y if row is None else y * gain_ref[row:row + 1, :]
            s_ref[...] = scaled.astype(s_ref.dtype)
        part = jnp.broadcast_to(jnp.sum(y * y, axis=-1, keepdims=True), out_ss_ref.shape)

        @pl.when(j == 0)
        def _():
            out_ss_ref[...] = part

        @pl.when(j > 0)
        def _():
            out_ss_ref[...] += part

    if nk == 1:
        y = product()
        if ss_dim:
            y = y * _inv_rms(in_ss_ref, ss_dim)
        if act == "relu2":
            y = jnp.maximum(y, 0.0)
            y = y * y
        if has_res:
            y = res_ref[...] + y
        o_ref[...] = y.astype(o_ref.dtype)
        if n_gain:
            emit_norm_inputs(y)
        return

    k = pl.program_id(2)

    @pl.when(k == 0)
    def _():
        o_ref[...] = res_ref[...] if has_res else jnp.zeros_like(o_ref)

    if not n_gain:
        o_ref[...] += product()
        return

    @pl.when(k < nk - 1)
    def _():
        o_ref[...] += product()

    @pl.when(k == nk - 1)
    def _():
        y = o_ref[...] + product()
        o_ref[...] = y
        emit_norm_inputs(y)


def _linear(lhs_list, w, n_out, out_dtype, *, res=None, act=None, row_ss=None, norm_gains=(),
            tm=1024, tn=1024, tk=4096, resident_w=False, side_casts=(), name="linear"):
    m = lhs_list[0].shape[0]
    k_sizes = [a.shape[1] for a in lhs_list]
    k_total = sum(k_sizes)
    assert w.shape[0] == k_total
    tm, tn = math.gcd(tm, m), math.gcd(tn, n_out)
    if len(lhs_list) > 1:
        assert k_total <= tk
        tk = k_total
    else:
        tk = math.gcd(tk, k_total)
        k_sizes = [tk]
    nk = k_total // tk
    assert nk == 1 or (act is None and out_dtype == F32 and row_ss is None)
    n_i, n_j = m // tm, n_out // tn
    n_gain = len(norm_gains)

    def tile(i, j, k):
        return (i, j)

    def row_block(i, j, k):
        return (i, 0)

    in_specs = [pl.BlockSpec((tm, ks), lambda i, j, k: (i, k)) for ks in k_sizes]
    if resident_w:
        assert nk == 1 and w.shape[1] == n_out
        in_specs.append(pl.BlockSpec(w.shape, lambda i, j, k: (0, 0),
                                     pipeline_mode=pl.Buffered(1)))
    else:
        in_specs.append(pl.BlockSpec((tk, tn), lambda i, j, k: (k, j)))
    args = list(lhs_list) + [w]
    if res is not None:
        in_specs.append(pl.BlockSpec((tm, tn), tile))
        args.append(res)
    if row_ss is not None:
        in_specs.append(pl.BlockSpec((tm, LANES), row_block))
        args.append(row_ss)
    gains = [g for g in norm_gains if g is not None]
    gain_rows = tuple(None if g is None else sum(h is not None for h in norm_gains[:n])
                      for n, g in enumerate(norm_gains))
    if gains:
        in_specs.append(pl.BlockSpec((len(gains), tn), lambda i, j, k: (0, j)))
        args.append(jnp.stack(gains).astype(F32))
    out_specs = [pl.BlockSpec((tm, tn), tile)] * (1 + n_gain)
    out_shape = [jax.ShapeDtypeStruct((m, n_out), out_dtype)]
    out_shape += [jax.ShapeDtypeStruct((m, n_out), BF16)] * n_gain
    if n_gain:
        out_specs.append(pl.BlockSpec((tm, LANES), row_block))
        out_shape.append(jax.ShapeDtypeStruct((m, LANES), F32))
    side_in, side_args, side_out, side_shapes, side_flags, finalize = _side_cast_plan(
        side_casts, n_i * n_j * nk, lambda i, j, k: (i * n_j + j) * nk + k)
    kernel = functools.partial(
        _linear_kernel, k_sizes=tuple(k_sizes), nk=nk, act=act, has_res=res is not None,
        ss_dim=k_total if row_ss is not None else 0, gain_rows=gain_rows, side_flags=side_flags,
        resident_tn=tn if resident_w else 0)
    n_main = len(out_specs)
    outs = pl.pallas_call(
        kernel,
        grid=(n_i, n_j, nk),
        in_specs=in_specs + side_in,
        out_specs=out_specs + side_out,
        out_shape=out_shape + side_shapes,
        compiler_params=_params("parallel", "arbitrary" if n_gain else "parallel", "arbitrary"),
        name=name,
    )(*args, *side_args)
    outs = list(outs[:n_main]) + finalize(outs[n_main:])
    return outs[0] if len(outs) == 1 else tuple(outs)


def _pool_kernel(h_ref, w_ref, wpg_ref, scale_ref, *rest, tm, side_flags):
    rest = list(rest)
    side_jobs = _pop_side_inputs(rest, side_flags)
    o_ref = rest.pop(0)
    u_ref, ub_ref, p_ref, carry_ref = rest[len(side_flags):]
    _run_side_casts(side_jobs, rest[:len(side_flags)])
    i = pl.program_id(1)
    g = pl.program_id(2)
    blk = POOL_BLOCK
    half = tm // 2
    u_ref[:half, :] = jnp.dot(h_ref[:half, :], w_ref[...], preferred_element_type=F32)
    u_ref[half:, :] = jnp.dot(h_ref[half:, :], w_ref[...], preferred_element_type=F32)
    ub_ref[:blk - POOL_HALO, :] = jnp.zeros((blk - POOL_HALO, ub_ref.shape[1]), ub_ref.dtype)

    @pl.when(i == 0)
    def _():
        ub_ref[blk - POOL_HALO:blk, :] = jnp.zeros((POOL_HALO, ub_ref.shape[1]), ub_ref.dtype)

    @pl.when(i > 0)
    def _():
        ub_ref[blk - POOL_HALO:blk, :] = carry_ref[g]

    ub_ref[blk:, :] = u_ref[...].astype(ub_ref.dtype)
    carry_ref[g] = ub_ref[blk + tm - POOL_HALO:, :]

    window = jnp.left_shift(2, g)
    row = lax.broadcasted_iota(jnp.int32, (blk, 2 * blk), 0) + blk
    col = lax.broadcasted_iota(jnp.int32, (blk, 2 * blk), 1)
    band = jnp.where((col <= row) & (col > row - window), 1.0, 0.0).astype(BF16)
    pos = i * tm + lax.broadcasted_iota(jnp.int32, (tm, 1), 0)
    inv_cnt = 1.0 / jnp.minimum(pos + 1, window).astype(F32)
    for r in range(tm // blk):
        rows = slice(r * blk, (r + 1) * blk)
        s = jnp.dot(band, ub_ref[r * blk:(r + 2) * blk, :], preferred_element_type=F32)
        p_ref[rows, :] = (s * inv_cnt[rows] - u_ref[rows, :]).astype(p_ref.dtype)

    mixed = jnp.dot(p_ref[...], wpg_ref[...], preferred_element_type=F32) * scale_ref[...]
    o_ref[...] = mixed.astype(o_ref.dtype)


def _pool_branch(h, w_in, w_pg, scale, batch, seq, tm=1024, side_casts=()):
    n, d = h.shape
    groups, gw = w_pg.shape[0], w_pg.shape[1]
    assert POOL_WINDOWS == tuple(2 << g for g in range(groups)) and max(POOL_WINDOWS) <= POOL_HALO
    tm = min(tm, seq)
    n_i = seq // tm
    side_in, side_args, side_out, side_shapes, side_flags, finalize = _side_cast_plan(
        side_casts, batch * n_i * groups, lambda b, i, g: (b * n_i + i) * groups + g)
    outs = pl.pallas_call(
        functools.partial(_pool_kernel, tm=tm, side_flags=side_flags),
        grid=(batch, n_i, groups),
        in_specs=[pl.BlockSpec((tm, d), lambda b, i, g: (b * n_i + i, 0)),
                  pl.BlockSpec((d, gw), lambda b, i, g: (0, g)),
                  pl.BlockSpec((None, gw, gw), lambda b, i, g: (g, 0, 0)),
                  pl.BlockSpec((1, gw), lambda b, i, g: (0, g))] + side_in,
        out_specs=[pl.BlockSpec((tm, gw), lambda b, i, g: (b * n_i + i, g))] + side_out,
        out_shape=[jax.ShapeDtypeStruct((n, groups * gw), BF16)] + side_shapes,
        scratch_shapes=[pltpu.VMEM((tm, gw), F32), pltpu.VMEM((POOL_BLOCK + tm, gw), BF16),
                        pltpu.VMEM((tm, gw), BF16), pltpu.VMEM((groups, POOL_HALO, gw), BF16)],
        compiler_params=_params("parallel", "arbitrary", "arbitrary"),
        name="pool_branch",
    )(h, w_in, w_pg, scale.reshape(1, -1), *side_args)
    return outs[0] if not side_casts else (outs[0], *finalize(outs[1:]))


def _mem_attn_kernel(h_ref, wq_ref, k_ref, v_ref, *rest, scale, ss_dim, hd):
    rest = list(rest)
    ss_ref = rest.pop(0) if ss_dim else None
    (o_ref,) = rest
    q = jnp.dot(h_ref[...], wq_ref[...], preferred_element_type=F32)
    if ss_dim:
        q = q * _inv_rms(ss_ref, ss_dim)
    q = q.astype(BF16)
    for head in range(MEM_HEADS):
        cols = slice(head * hd, (head + 1) * hd)
        s = lax.dot_general(q[:, cols], k_ref[:, cols], (((1,), (1,)), ((), ())),
                            preferred_element_type=F32) * scale
        m = jnp.max(s, axis=-1, keepdims=True)
        p = jnp.exp(s - m)
        inv_den = 1.0 / jnp.sum(p, axis=-1, keepdims=True)
        o = jnp.dot(p.astype(BF16), v_ref[:, cols], preferred_element_type=F32) * inv_den
        o_ref[:, cols] = o.astype(o_ref.dtype)


def _mem_attn(h, w_in, q_col0, mkv, batch, seq, mem_len, tm=1024, row_ss=None):
    n, d = h.shape
    width = mkv.shape[1] // 2
    hd = width // MEM_HEADS
    tm = min(tm, seq)
    n_i = seq // tm
    assert q_col0 % width == 0
    in_specs = [pl.BlockSpec((tm, d), lambda b, i: (b * n_i + i, 0)),
                pl.BlockSpec((d, width), lambda b, i: (0, q_col0 // width)),
                pl.BlockSpec((mem_len, width), lambda b, i: (b, 0)),
                pl.BlockSpec((mem_len, width), lambda b, i: (b, 1))]
    args = [h, w_in, mkv, mkv]
    if row_ss is not None:
        in_specs.append(pl.BlockSpec((tm, LANES), lambda b, i: (b * n_i + i, 0)))
        args.append(row_ss)
    return pl.pallas_call(
        functools.partial(_mem_attn_kernel, scale=1.0 / math.sqrt(hd),
                          ss_dim=d if row_ss is not None else 0, hd=hd),
        grid=(batch, n_i),
        in_specs=in_specs,
        out_specs=pl.BlockSpec((tm, width), lambda b, i: (b * n_i + i, 0)),
        out_shape=jax.ShapeDtypeStruct((n, width), BF16),
        compiler_params=_params("parallel", "parallel"),
        name="mem_attn",
    )(*args)


def _t5_bucket(dist):
    max_exact = NUM_BUCKETS // 2
    d32 = jnp.maximum(dist, 1).astype(F32)
    large = max_exact + (jnp.log(d32 / max_exact) / math.log(MAX_DISTANCE / max_exact)
                         * (NUM_BUCKETS - max_exact)).astype(jnp.int32)
    large = jnp.minimum(large, NUM_BUCKETS - 1)
    return jnp.where(dist < max_exact, dist, large)


def _bucket_tables():
    qi = jnp.arange(ATT_BLOCK)[:, None]
    kj = jnp.arange(2 * ATT_BLOCK)[None, :]
    delta = qi + ATT_BLOCK - kj
    band = (delta >= 0) & (delta <= ATT_BLOCK)
    tabs = [jnp.where(band, _t5_bucket(jnp.maximum(delta, 0) * dil), -1) for _, dil in DIL_CONFIGS]
    return jnp.stack(tabs).astype(jnp.int32)


def _bias_table_kernel(rb_ref, bkt_ref, o_ref):
    gh = pl.program_id(0)
    bkt = bkt_ref[...]
    bias = jnp.full(bkt.shape, MASKED, F32)
    for b in range(NUM_BUCKETS):
        bias = jnp.where(bkt == b, rb_ref[b, gh], bias)
    o_ref[...] = bias


def _bias_tables(rel_bias, heads):
    n_gh = rel_bias.shape[1]
    tab = (ATT_BLOCK, 2 * ATT_BLOCK)
    return pl.pallas_call(
        _bias_table_kernel,
        grid=(n_gh,),
        in_specs=[pl.BlockSpec(memory_space=pltpu.SMEM),
                  pl.BlockSpec((None,) + tab, lambda gh: (gh // heads, 0, 0))],
        out_specs=pl.BlockSpec((None,) + tab, lambda gh: (gh, 0, 0)),
        out_shape=jax.ShapeDtypeStruct((n_gh,) + tab, F32),
        compiler_params=_params("parallel"),
        name="bias_tables",
    )(rel_bias.astype(F32), _bucket_tables())


def _dil_attn_kernel(b0, b1, b2, q0, q1, q2, k0, k1, k2, v0, v1, v2, o_ref,
                     og0, og1, og2, lg0, lg1, lg2, *, seq, scale):
    bias_refs = (b0, b1, b2)
    q_refs, k_refs, v_refs = (q0, q1, q2), (k0, k1, k2), (v0, v1, v2)
    o_nat, lse_nat = (og0, og1, og2), (lg0, lg1, lg2)
    blk = ATT_BLOCK

    def qk(q3, k3):
        return jnp.einsum("uqd,ukd->uqk", q3, k3, preferred_element_type=F32) * scale

    def pv(p3, v3):
        return jnp.einsum("uqk,ukd->uqd", p3.astype(BF16), v3, preferred_element_type=F32)

    for g, (_, dil) in enumerate(DIL_CONFIGS):
        sub_len = seq // dil
        n_blk = sub_len // blk
        units = dil * n_blk

        def blocks(ref):
            parts = []
            for r in range(dil):
                rows = pl.ds(r, sub_len, stride=dil) if dil > 1 else pl.ds(0, sub_len)
                parts.append(ref[rows, :].astype(BF16).reshape(n_blk, blk, HEAD_DIM))
            return parts[0] if dil == 1 else jnp.concatenate(parts, axis=0)

        q3, k3, v3 = blocks(q_refs[g]), blocks(k_refs[g]), blocks(v_refs[g])
        bias = bias_refs[g][...]
        s_cur = qk(q3, k3) + bias[None, :, blk:]
        m = jnp.max(s_cur, axis=-1, keepdims=True)
        if n_blk > 1:
            k_prev = jnp.concatenate([k3[:1], k3[:-1]], axis=0)
            v_prev = jnp.concatenate([v3[:1], v3[:-1]], axis=0)
            unit = lax.broadcasted_iota(jnp.int32, (units, 1, 1), 0)
            bias_prev = jnp.where(unit % n_blk == 0, MASKED, bias[None, :, :blk])
            s_prev = qk(q3, k_prev) + bias_prev
            m = jnp.maximum(m, jnp.max(s_prev, axis=-1, keepdims=True))
        ones = jnp.ones_like(v3)
        acc = pv(jnp.exp(s_cur - m), jnp.concatenate([v3, ones], axis=-1))
        if n_blk > 1:
            acc = acc + pv(jnp.exp(s_prev - m), jnp.concatenate([v_prev, ones], axis=-1))
        den = acc[:, :, HEAD_DIM:]
        o3 = acc[:, :, :HEAD_DIM] * (1.0 / den)
        lse3 = m + jnp.log(den)
        for u in range(units):
            r, n = divmod(u, n_blk)
            start = r + dil * n * blk
            nat = pl.ds(start, blk, stride=dil) if dil > 1 else pl.ds(start, blk)
            o_nat[g][nat, :] = o3[u]
            lse_nat[g][nat, :] = jnp.broadcast_to(lse3[u], (blk, HEAD_DIM))

    l0, l1, l2 = lse_nat[0][...], lse_nat[1][...], lse_nat[2][...]
    top = jnp.maximum(jnp.maximum(l0, l1), l2)
    w0, w1, w2 = jnp.exp(l0 - top), jnp.exp(l1 - top), jnp.exp(l2 - top)
    inv = 1.0 / (w0 + w1 + w2)
    out = (o_nat[0][...] * w0 + o_nat[1][...] * w1 + o_nat[2][...] * w2) * inv
    o_ref[...] = out.astype(o_ref.dtype)


def _dil_attn(q, kv, rel_bias, batch, seq):
    n = q.shape[0]
    n_groups = len(DIL_CONFIGS)
    heads = rel_bias.shape[1] // n_groups
    gh = n_groups * heads
    bias = _bias_tables(rel_bias, heads)

    def col_spec(off):
        return pl.BlockSpec((seq, HEAD_DIM), lambda b, hh: (b, off + hh))

    in_specs = [pl.BlockSpec((None, ATT_BLOCK, 2 * ATT_BLOCK),
                             lambda b, hh, g=g: (g * heads + hh, 0, 0)) for g in range(n_groups)]
    in_specs += [col_spec(g * heads) for g in range(n_groups)]
    in_specs += [col_spec(g * heads) for g in range(n_groups)]
    in_specs += [col_spec(gh + g * heads) for g in range(n_groups)]
    return pl.pallas_call(
        functools.partial(_dil_attn_kernel, seq=seq, scale=1.0 / math.sqrt(HEAD_DIM)),
        grid=(batch, heads),
        in_specs=in_specs,
        out_specs=pl.BlockSpec((seq, HEAD_DIM), lambda b, hh: (b, hh)),
        out_shape=jax.ShapeDtypeStruct((n, heads * HEAD_DIM), BF16),
        scratch_shapes=[pltpu.VMEM((seq, HEAD_DIM), F32)] * (2 * n_groups),
        compiler_params=_params("parallel", "arbitrary"),
        name="dil_attn",
    )(bias, bias, bias, q, q, q, kv, kv, kv, kv, kv, kv)


def kernel(x, mem, a_norm, a_w_in, a_w_pg, a_scale, a_w_out, kv_norm, w_kv, b_norm, b_w_in,
           b_w_out, mem_norm, w_mem_kv, mlp_norm, mlp_w1, mlp_w2, rel_bias, final_norm):
    batch, seq, d = x.shape
    mem_len = mem.shape[1]
    n = batch * seq
    pool_width = a_scale.shape[-1]
    dil_q_width = w_kv.shape[1] // 2
    assert a_norm.shape[0] == 1 and b_norm.shape[0] == 1 and mlp_norm.shape[0] == 2

    x0 = x.reshape(n, d)
    (mem_h,) = _rmsnorm(mem.reshape(batch * mem_len, d), [mem_norm], BF16)
    groups, gw = a_w_pg.shape[1], a_w_pg.shape[2]
    h0, mkv, (a_in, a_pg) = _norm_and_mem_kv(
        x0, a_norm[0], mem_h, w_mem_kv,
        side_casts=[(0, a_w_in), (0, a_w_pg.reshape(1, groups * gw, gw))])

    mixed, a_out_w, b_out_w, w1 = _pool_branch(
        h0, a_in, a_pg.reshape(groups, gw, gw), a_scale[0], batch, seq,
        side_casts=[(0, a_w_out), (0, b_w_out), (0, mlp_w1)])
    mem_out = _mem_attn(h0, a_in, pool_width, mkv[0], batch, seq, mem_len)
    x1, xg, ss = _linear([mixed, mem_out], a_out_w, d, F32, res=x0, norm_gains=[mlp_norm[0]],
                         name="a_out")
    hid, w2, w1_next, kv_w, b_in = _linear(
        [xg], w1, mlp_w1.shape[2], BF16, act="relu2", row_ss=ss,
        side_casts=[(0, mlp_w2), (1, mlp_w1), (0, w_kv[None], kv_norm), (0, b_w_in, b_norm[0])],
        name="mlp_up")
    x2, xn, ss = _linear([hid], w2, d, F32, res=x1, norm_gains=[None], name="mlp_down")

    kv = _linear([xn], kv_w, w_kv.shape[1], F32, row_ss=ss, name="kv_proj")
    q = _linear([xn], b_in, dil_q_width, F32, row_ss=ss, name="q_proj")
    dil_out = _dil_attn(q, kv, rel_bias, batch, seq)
    mem_out = _mem_attn(xn, b_in, dil_q_width, mkv[1], batch, seq, mem_len, row_ss=ss)
    x3, xg, ss = _linear([dil_out, mem_out], b_out_w, d, F32, res=x2, norm_gains=[mlp_norm[1]],
                         resident_w=True, name="b_out")
    hid, w2 = _linear([xg], w1_next, mlp_w1.shape[2], BF16, act="relu2", row_ss=ss,
                      side_casts=[(1, mlp_w2)], name="mlp_up")
    x4 = _linear([hid], w2, d, F32, res=x3, name="mlp_down")

    (out,) = _rmsnorm(x4, [final_norm], F32)
    return out.reshape(batch, seq, d)
```
